```python
import math, functools
import jax
import jax.numpy as jnp
from jax import lax
import numpy as np

D_MODEL = 2048
BATCH = 2
SEQ = 4096
DEPTH = 4
DEC_BATCH = 8
DEC_SEQ = 1
PAST_LEN = 16384
PAGE_SIZE = 128

GROUP_W = D_MODEL // 4
MIX_W = 4 * GROUP_W
A_HEADS = 4
A_HD = GROUP_W // A_HEADS
IDX_HEADS = 8
IDX_DIM = 64
TOPK_MAX = 256
Q_BLOCK = 128
R_HEADS = 4
R_DV = GROUP_W // R_HEADS
R_DK = R_DV // 2
R_QK = R_HEADS * R_DK
ROPE_BASE = 10000.0
M_HEADS = 4
M_DV = GROUP_W // M_HEADS
M_DK = M_DV // 2
M_QK = M_HEADS * M_DK
CONV_W = 4
G_HEADS = 4
G_DV = GROUP_W // G_HEADS
G_DK = G_DV // 2
G_QK = G_HEADS * G_DK
G_LOWRANK = 16
G_TAU = 16.0
CHUNK = 128
PLE_DIM = 256
EPS = 1e-6

IN_SPLITS = (
    ('a_q', A_HEADS * A_HD), ('a_k', A_HEADS * A_HD), ('a_v', A_HEADS * A_HD),
    ('a_iq', IDX_HEADS * IDX_DIM), ('a_ik', IDX_DIM), ('a_iw', IDX_HEADS), ('a_z', GROUP_W),
    ('r_q', R_QK), ('r_k', R_QK), ('r_v', GROUP_W), ('r_z', GROUP_W),
    ('m_q', M_QK), ('m_k', M_QK), ('m_v', GROUP_W), ('m_i', M_HEADS), ('m_f', M_HEADS),
    ('m_o', GROUP_W), ('m_z', GROUP_W),
    ('g_q', G_QK), ('g_k', G_QK), ('g_v', GROUP_W), ('g_lr', G_LOWRANK), ('g_z', GROUP_W),
)
N_IN = sum(w for _, w in IN_SPLITS)

kernel_name = 'hybrid_dsa_retention_mlstm_gla_step'


def split_cols(u):
    out = {}
    off = 0
    for name, width in IN_SPLITS:
        out[name] = u[..., off:off + width]
        off += width
    return out


def rmsnorm(x, g):
    xf = x.astype(jnp.float32)
    y = xf * lax.rsqrt(jnp.mean(xf * xf, axis=-1, keepdims=True) + EPS)
    return (y * g.astype(jnp.float32)).astype(x.dtype)


def head_norm(o, g, center):
    B, T, H, d = o.shape
    of = o.astype(jnp.float32)
    if center:
        of = of - jnp.mean(of, axis=-1, keepdims=True)
    of = of * lax.rsqrt(jnp.mean(of * of, axis=-1, keepdims=True) + EPS)
    return (of.reshape(B, T, H * d) * g.astype(jnp.float32)).astype(o.dtype)


def rope(x, pos):
    d = x.shape[-1]
    half = d // 2
    inv = jnp.exp(-math.log(ROPE_BASE) * jnp.arange(half, dtype=jnp.float32) * (2.0 / d))
    ang = pos.astype(jnp.float32)[:, None] * inv[None, :]
    cos = jnp.cos(ang)[None, :, None, :]
    sin = jnp.sin(ang)[None, :, None, :]
    xf = x.astype(jnp.float32)
    x1, x2 = xf[..., :half], xf[..., half:]
    return jnp.concatenate([x1 * cos - x2 * sin, x1 * sin + x2 * cos], axis=-1).astype(x.dtype)


def chunk_len(T):
    return CHUNK if T % CHUNK == 0 else T


def to_chunks(a, C):
    n = a.shape[2] // C
    return jnp.moveaxis(a.reshape(a.shape[:2] + (n, C) + a.shape[3:]), 2, 0)


def from_chunks(o):
    o = jnp.moveaxis(o, 0, 2)
    return o.reshape(o.shape[:2] + (o.shape[2] * o.shape[3],) + o.shape[4:])


def retention(q, k, v, S0):
    T = q.shape[2]
    C = chunk_len(T)
    log_g = jnp.log1p(-jnp.exp2(-5.0 - jnp.arange(R_HEADS, dtype=jnp.float32)))
    idx = jnp.arange(C, dtype=jnp.float32)
    diff = idx[:, None] - idx[None, :]
    causal = diff >= 0
    dmat = jnp.where(causal[None], jnp.exp(jnp.where(causal, diff, 0.0)[None] * log_g[:, None, None]), 0.0)
    inter = jnp.exp((idx + 1.0)[None, :] * log_g[:, None])
    kdec = jnp.exp((C - 1.0 - idx)[None, :] * log_g[:, None])
    g_c = jnp.exp(C * log_g)

    def step(S, xs):
        qc, kc, vc = xs
        att = jnp.einsum('bhtd,bhsd->bhts', qc, kc) * dmat[None]
        o = jnp.einsum('bhts,bhsv->bhtv', att, vc) + jnp.einsum('bhtd,bhdv->bhtv', qc, S) * inter[None, :, :, None]
        S = S * g_c[None, :, None, None] + jnp.einsum('bhsd,bhsv->bhdv', kc * kdec[None, :, :, None], vc)
        return S, o

    f = lambda a: to_chunks(a.astype(jnp.float32), C)
    S, o = lax.scan(step, S0.astype(jnp.float32), (f(q), f(k), f(v)))
    return from_chunks(o).astype(v.dtype), S.astype(S0.dtype)


def mlstm(q, k, v, i_pre, log_f, C0, n0, m0):
    T = q.shape[2]
    C = chunk_len(T)
    tri = jnp.arange(C)[:, None] >= jnp.arange(C)[None, :]

    def step(carry, xs):
        Cs, ns, ms = carry
        qc, kc, vc, ic, fc = xs
        b = jnp.cumsum(fc, axis=-1)
        dlog = jnp.where(tri, b[..., :, None] - b[..., None, :] + ic[..., None, :], -jnp.inf)
        m_inter = b + ms[..., None]
        m_t = jnp.maximum(m_inter, jnp.max(dlog, axis=-1))
        wts = jnp.exp(dlog - m_t[..., None])
        sc = jnp.exp(m_inter - m_t)
        qk = jnp.einsum('bhtd,bhsd->bhts', qc, kc) * wts
        num = jnp.einsum('bhts,bhsv->bhtv', qk, vc) + sc[..., None] * jnp.einsum('bhvd,bhtd->bhtv', Cs, qc)
        dot = jnp.sum(qk, axis=-1) + sc * jnp.einsum('bhd,bhtd->bht', ns, qc)
        den = jnp.maximum(jnp.abs(dot), jnp.exp(-m_t))
        h = num / den[..., None]
        m_new = m_t[..., -1]
        s_last = sc[..., -1]
        w_last = wts[..., -1, :]
        C_new = s_last[..., None, None] * Cs + jnp.einsum('bhs,bhsv,bhsd->bhvd', w_last, vc, kc)
        n_new = s_last[..., None] * ns + jnp.einsum('bhs,bhsd->bhd', w_last, kc)
        return (C_new, n_new, m_new), h

    f = lambda a: to_chunks(a.astype(jnp.float32), C)
    init = (C0.astype(jnp.float32), n0.astype(jnp.float32), m0.astype(jnp.float32))
    (Cf, nf, mf), h = lax.scan(step, init, (f(q), f(k), f(v), f(i_pre), f(log_f)))
    return from_chunks(h).astype(v.dtype), Cf.astype(C0.dtype), nf.astype(n0.dtype), mf.astype(m0.dtype)


def gla(q, k, v, log_a, S0):
    T = q.shape[2]
    C = chunk_len(T)
    tri = (jnp.arange(C)[:, None] >= jnp.arange(C)[None, :])[None, None, :, :, None]

    def step(S, xs):
        qc, kc, vc, ac = xs
        b = jnp.cumsum(ac, axis=2)
        rel = jnp.where(tri, b[:, :, :, None, :] - b[:, :, None, :, :], -jnp.inf)
        att = jnp.einsum('bhtd,bhsd,bhtsd->bhts', qc, kc, jnp.exp(rel))
        o = jnp.einsum('bhts,bhsv->bhtv', att, vc) + jnp.einsum('bhtd,bhdv->bhtv', qc * jnp.exp(b), S)
        b_last = b[:, :, -1]
        S = jnp.exp(b_last)[..., None] * S + jnp.einsum('bhsd,bhsv->bhdv', kc * jnp.exp(b_last[:, :, None, :] - b), vc)
        return S, o

    f = lambda a: to_chunks(a.astype(jnp.float32), C)
    S, o = lax.scan(step, S0.astype(jnp.float32), (f(q), f(k), f(v), f(log_a)))
    return from_chunks(o).astype(v.dtype), S.astype(S0.dtype)


def causal_conv(u, buf, w, b):
    T = u.shape[1]
    cat = jnp.concatenate([buf.astype(u.dtype), u], axis=1)
    y = b
    for j in range(CONV_W):
        y = y + cat[:, j:j + T] * w[j]
    return y, cat[:, T:]


def indexer_select(qi, wi, ki, qpos, topk):
    L = ki.shape[1]
    s = jnp.einsum('bqhd,bld->bqhl', qi.astype(jnp.float32), ki.astype(jnp.float32)) * (IDX_DIM ** -0.5)
    score = jnp.einsum('bqhl,bqh->bql', jax.nn.relu(s), wi.astype(jnp.float32)) * (IDX_HEADS ** -0.5)
    vis = jnp.arange(L)[None, None, :] <= qpos[None, :, None]
    score = jnp.where(vis, score, -jnp.inf)
    _, idx = lax.top_k(score, topk)
    return idx, idx <= qpos[None, :, None]


def sparse_softmax(q, ksel, vsel, valid):
    logits = jnp.einsum('bqhd,bqkhd->bqhk', q.astype(jnp.float32), ksel.astype(jnp.float32)) * (A_HD ** -0.5)
    logits = jnp.where(valid[:, :, None, :], logits, -jnp.inf)
    p = jax.nn.softmax(logits, axis=-1)
    return jnp.einsum('bqhk,bqkhd->bqhd', p, vsel.astype(jnp.float32)).astype(q.dtype)


take_rows = jax.vmap(lambda a, i: a[i])


def dsa_prompt(q, k, v, qi, ki, wi):
    B, T = q.shape[:2]
    topk = min(TOPK_MAX, T // 4)
    nb = T // Q_BLOCK

    def blocks(a):
        return jnp.moveaxis(a.reshape((B, nb, Q_BLOCK) + a.shape[2:]), 1, 0)

    def one_block(args):
        qb, qib, wib, start = args
        qpos = start + jnp.arange(Q_BLOCK)
        idx, valid = indexer_select(qib, wib, ki, qpos, topk)
        return sparse_softmax(qb, take_rows(k, idx), take_rows(v, idx), valid)

    out = lax.map(one_block, (blocks(q), blocks(qi), blocks(wi), jnp.arange(nb) * Q_BLOCK))
    return jnp.moveaxis(out, 0, 1).reshape(q.shape)


def dsa_sample(q, k, v, qi, ki, wi, *, cache_k, cache_v, cache_kidx, page_table, layer):
    B, Tn = q.shape[:2]
    past = page_table.shape[1] * PAGE_SIZE
    topk = min(TOPK_MAX, (past + Tn) // 4)
    ki_past = cache_kidx[layer, page_table].reshape(B, past, IDX_DIM).astype(ki.dtype)
    ki_all = jnp.concatenate([ki_past, ki], axis=1)
    qpos = past + jnp.arange(Tn)
    idx, valid = indexer_select(qi, wi, ki_all, qpos, topk)
    pidx = jnp.minimum(idx, past - 1)
    page = page_table[jnp.arange(B)[:, None, None], pidx // PAGE_SIZE]
    off = pidx % PAGE_SIZE
    nidx = jnp.clip(idx - past, 0, Tn - 1)
    in_past = (idx < past)[..., None, None]
    ksel = jnp.where(in_past, cache_k[layer, page, off].astype(k.dtype), take_rows(k, nidx))
    vsel = jnp.where(in_past, cache_v[layer, page, off].astype(v.dtype), take_rows(v, nidx))
    return sparse_softmax(q, ksel, vsel, valid)


def mixer_layer(h, pl, lw, pos, attend, st):
    (g_norm, w_in, conv_w, conv_b, i_b, f_b, wg2, bg2, rg, mg, gg, w_out, ple_w, ple_gw) = lw
    ret_s, m_c, m_n, m_m, conv_buf, gla_s = st
    B, T, _ = h.shape
    c = split_cols(rmsnorm(h, g_norm) @ w_in)

    def heads(a, n):
        return a.reshape(B, T, n, -1)

    def bhtd(a):
        return jnp.swapaxes(a, 1, 2)

    silu = jax.nn.silu
    a_k = heads(c['a_k'], A_HEADS)
    a_v = heads(c['a_v'], A_HEADS)
    a_o = attend(heads(c['a_q'], A_HEADS), a_k, a_v, heads(c['a_iq'], IDX_HEADS), c['a_ik'], c['a_iw'])
    a_out = a_o.reshape(B, T, GROUP_W)
    r_q = rope(heads(c['r_q'], R_HEADS), pos) * (R_DK ** -0.5)
    r_k = rope(heads(c['r_k'], R_HEADS), pos)
    r_o, ret_s = retention(bhtd(r_q), bhtd(r_k), bhtd(heads(c['r_v'], R_HEADS)), ret_s)
    r_out = head_norm(bhtd(r_o), rg, True)
    qk, conv_buf = causal_conv(jnp.concatenate([c['m_q'], c['m_k']], axis=-1), conv_buf, conv_w, conv_b)
    qk = silu(qk)
    m_q = heads(qk[..., :M_QK], M_HEADS)
    m_k = heads(qk[..., M_QK:], M_HEADS) * (M_DK ** -0.5)
    i_pre = jnp.swapaxes(c['m_i'] + i_b, 1, 2)
    log_f = jnp.swapaxes(jax.nn.log_sigmoid(c['m_f'] + f_b), 1, 2)
    m_o, m_c, m_n, m_m = mlstm(bhtd(m_q), bhtd(m_k), bhtd(heads(c['m_v'], M_HEADS)), i_pre, log_f, m_c, m_n, m_m)
    m_h = bhtd(m_o) * jax.nn.sigmoid(heads(c['m_o'], M_HEADS))
    m_out = head_norm(m_h, mg, True)
    log_a = jax.nn.log_sigmoid(c['g_lr'] @ wg2 + bg2) / G_TAU
    g_o, gla_s = gla(bhtd(heads(c['g_q'], G_HEADS)) * (G_DK ** -0.5), bhtd(heads(c['g_k'], G_HEADS)),
                     bhtd(heads(c['g_v'], G_HEADS)), bhtd(heads(log_a, G_HEADS)), gla_s)
    g_out = head_norm(bhtd(g_o), gg, False)
    mixed = jnp.concatenate([a_out * silu(c['a_z']), r_out * silu(c['r_z']),
                             m_out * silu(c['m_z']), g_out * silu(c['g_z'])], axis=-1)
    h = h + mixed @ w_out
    h = h + jax.nn.sigmoid(h @ ple_gw) * (pl @ ple_w)
    return h, (a_k, a_v, c['a_ik'], ret_s, m_c, m_n, m_m, conv_buf, gla_s)


def setup_inputs(seed: int = 0) -> dict:
    key = jax.random.key(seed)
    k = jax.random.split(key, 32)

    def nrm(i, shape, scale=1.0):
        return jax.random.normal(k[i], shape, jnp.float32) * scale

    n_pages = PAST_LEN // PAGE_SIZE
    n_used = DEC_BATCH * n_pages
    n_pool = n_used + max(1, n_used // 4)
    page_table = jax.random.permutation(k[0], n_pool)[:n_used].reshape(DEC_BATCH, n_pages).astype(jnp.int32)
    return {
        'x_prompt': nrm(1, (BATCH, SEQ, D_MODEL)),
        'x_sample': nrm(2, (DEC_BATCH, DEC_SEQ, D_MODEL)),
        'cache_k': nrm(3, (DEPTH, n_pool, PAGE_SIZE, A_HEADS, A_HD)),
        'cache_v': nrm(4, (DEPTH, n_pool, PAGE_SIZE, A_HEADS, A_HD)),
        'cache_kidx': nrm(5, (DEPTH, n_pool, PAGE_SIZE, IDX_DIM)),
        'page_table': page_table,
        'state_ret': nrm(6, (DEPTH, DEC_BATCH, R_HEADS, R_DK, R_DV)),
        'state_mlstm_c': nrm(7, (DEPTH, DEC_BATCH, M_HEADS, M_DV, M_DK), 0.5),
        'state_mlstm_n': nrm(8, (DEPTH, DEC_BATCH, M_HEADS, M_DK), 0.5),
        'state_mlstm_m': nrm(9, (DEPTH, DEC_BATCH, M_HEADS)),
        'state_mlstm_conv': nrm(10, (DEPTH, DEC_BATCH, CONV_W - 1, 2 * M_QK)),
        'state_gla': nrm(11, (DEPTH, DEC_BATCH, G_HEADS, G_DK, G_DV)),
        'p_prompt': nrm(12, (DEPTH, BATCH, SEQ, PLE_DIM)),
        'p_sample': nrm(13, (DEPTH, DEC_BATCH, DEC_SEQ, PLE_DIM)),
        'norm_g': 1.0 + nrm(14, (DEPTH, D_MODEL), 0.02),
        'w_in': nrm(15, (DEPTH, D_MODEL, N_IN), D_MODEL ** -0.5),
        'conv_w': nrm(16, (DEPTH, CONV_W, 2 * M_QK), CONV_W ** -0.5),
        'conv_b': nrm(17, (DEPTH, 2 * M_QK), 0.02),
        'm_igate_b': nrm(18, (DEPTH, M_HEADS), 0.1),
        'm_fgate_b': jnp.linspace(3.0, 6.0, M_HEADS, dtype=jnp.float32)[None, :] + nrm(19, (DEPTH, M_HEADS), 0.1),
        'gla_wg2': nrm(20, (DEPTH, G_LOWRANK, G_QK), G_LOWRANK ** -0.5),
        'gla_bg2': nrm(21, (DEPTH, G_QK), 0.02),
        'ret_norm_g': 1.0 + nrm(22, (DEPTH, GROUP_W), 0.02),
        'mlstm_norm_g': 1.0 + nrm(23, (DEPTH, GROUP_W), 0.02),
        'gla_norm_g': 1.0 + nrm(24, (DEPTH, GROUP_W), 0.02),
        'w_out': nrm(25, (DEPTH, MIX_W, D_MODEL), MIX_W ** -0.5),
        'ple_w': nrm(26, (DEPTH, PLE_DIM, D_MODEL), PLE_DIM ** -0.5),
        'ple_gate_w': nrm(27, (DEPTH, D_MODEL, D_MODEL), D_MODEL ** -0.5),
        'final_g': 1.0 + nrm(28, (D_MODEL,), 0.02),
    }


def reference(x_prompt, x_sample, cache_k, cache_v, cache_kidx, page_table, state_ret, state_mlstm_c,
              state_mlstm_n, state_mlstm_m, state_mlstm_conv, state_gla, p_prompt, p_sample,
              norm_g, w_in, conv_w, conv_b, m_igate_b, m_fgate_b, gla_wg2, gla_bg2,
              ret_norm_g, mlstm_norm_g, gla_norm_g, w_out, ple_w, ple_gate_w, final_g):
    B_p, T_p = x_prompt.shape[:2]
    past = page_table.shape[1] * PAGE_SIZE
    pos_p = jnp.arange(T_p)
    pos_s = past + jnp.arange(x_sample.shape[1])
    dt = x_prompt.dtype
    zero_p = (jnp.zeros((B_p, R_HEADS, R_DK, R_DV), dt), jnp.zeros((B_p, M_HEADS, M_DV, M_DK), dt),
              jnp.zeros((B_p, M_HEADS, M_DK), dt), jnp.zeros((B_p, M_HEADS), dt),
              jnp.zeros((B_p, CONV_W - 1, 2 * M_QK), dt), jnp.zeros((B_p, G_HEADS, G_DK, G_DV), dt))
    hp, hs = x_prompt, x_sample
    outs_p, outs_s = [], []
    for l in range(DEPTH):
        lw = (norm_g[l], w_in[l], conv_w[l], conv_b[l], m_igate_b[l], m_fgate_b[l], gla_wg2[l], gla_bg2[l],
              ret_norm_g[l], mlstm_norm_g[l], gla_norm_g[l], w_out[l], ple_w[l], ple_gate_w[l])
        hp, sp = mixer_layer(hp, p_prompt[l], lw, pos_p, dsa_prompt, zero_p)
        attend_s = functools.partial(dsa_sample, cache_k=cache_k, cache_v=cache_v, cache_kidx=cache_kidx,
                                     page_table=page_table, layer=l)
        st_s = (state_ret[l], state_mlstm_c[l], state_mlstm_n[l], state_mlstm_m[l], state_mlstm_conv[l], state_gla[l])
        hs, ss = mixer_layer(hs, p_sample[l], lw, pos_s, attend_s, st_s)
        outs_p.append(sp)
        outs_s.append(ss)
    y_prompt = rmsnorm(hp, final_g)
    y_sample = rmsnorm(hs, final_g)
    kp, vp, kip, rp, mcp, mnp, mmp, cvp, gp = [jnp.stack([o[j] for o in outs_p]) for j in range(9)]
    ks, vs, kis, rs, mcs, mns, mms, cvs, gs = [jnp.stack([o[j] for o in outs_s]) for j in range(9)]
    return (y_prompt, y_sample, kp, vp, kip, rp, mcp, mnp, mmp, cvp, gp, ks, vs, kis, rs, mcs, mns, mms, cvs, gs)
```

```python
import functools
import math

import numpy as np
import jax
import jax.numpy as jnp
from jax import lax
from jax.experimental import pallas as pl
from jax.experimental.pallas import tpu as pltpu

F32 = jnp.float32
I32 = jnp.int32
MXU_DT = jnp.bfloat16
NEG_INF = float("-inf")
HIGHEST = lax.Precision.HIGHEST

D_MODEL = 2048
DEPTH = 4
PAGE = 128
GROUP_W = 512
HEADS = 4
HD = 128
DK = 64
IDX_HEADS = 8
IDX_DIM = 64
TOPK_MAX = 256
CONV_W = 4
LOWRANK = 16
G_TAU = 16.0
ROPE_BASE = 10000.0
CHUNK = 128
PLE_DIM = 256
EPS = 1e-6
LANES = 128
VMEM_LIMIT = 56 * 1024 * 1024

_IN_SPLITS = (
    ('a_q', 512), ('a_k', 512), ('a_v', 512), ('a_iq', 512), ('a_ik', 64), ('a_iw', 8), ('a_z', 512),
    ('r_q', 256), ('r_k', 256), ('r_v', 512), ('r_z', 512),
    ('m_q', 256), ('m_k', 256), ('m_v', 512), ('m_i', 4), ('m_f', 4), ('m_o', 512), ('m_z', 512),
    ('g_q', 256), ('g_k', 256), ('g_v', 512), ('g_lr', 16), ('g_z', 512),
)
_MAIN = ('a_q', 'a_k', 'a_v', 'a_iq', 'a_z', 'r_q', 'r_k', 'r_v', 'r_z',
         'm_q', 'm_k', 'm_v', 'm_o', 'm_z', 'g_q', 'g_k', 'g_v', 'g_z')
_SMALL = ('a_ik', 'a_iw', 'm_i', 'm_f', 'g_lr')


def _layout():
    orig, off = {}, 0
    for name, w in _IN_SPLITS:
        orig[name] = (off, w)
        off += w
    new, off = {}, 0
    for name in _MAIN + _SMALL:
        new[name] = off
        off += orig[name][1]
    return orig, new, off


_ORIG, COL, _USED = _layout()
N_U = 8192
SMALL0 = COL['a_ik']
assert SMALL0 % LANES == 0 and _USED - SMALL0 <= LANES
S_IK, S_IW, S_MI, S_MF, S_LR = (COL[n] - SMALL0 for n in _SMALL)


def _dot(a, b, **kw):
    return jnp.dot(a, b, preferred_element_type=F32, **kw)


def _dot_nt(a, b):
    return lax.dot_general(a, b, (((1,), (1,)), ((), ())), preferred_element_type=F32)


def _dot_tn(a, b):
    return lax.dot_general(a, b, (((0,), (0,)), ((), ())), preferred_element_type=F32)


def _mx(x):
    return x.astype(MXU_DT)


def _cparams(sem):
    return pltpu.CompilerParams(dimension_semantics=sem, vmem_limit_bytes=VMEM_LIMIT)


def _const_spec(shape):
    nd = len(shape)
    return pl.BlockSpec(shape, lambda *_: (0,) * nd)


def _single(shape, index_map):
    return pl.BlockSpec(shape, index_map, pipeline_mode=pl.Buffered(1))


def _rms_kernel(x_ref, g_ref, o_ref):
    x = x_ref[...]
    y = x * lax.rsqrt(jnp.mean(x * x, axis=-1, keepdims=True) + EPS)
    o_ref[...] = (y * g_ref[...]).astype(o_ref.dtype)


def _rmsnorm(x, g, out_dtype):
    m = x.shape[0]
    tm = min(m, 512)
    return pl.pallas_call(
        _rms_kernel, grid=(m // tm,),
        in_specs=[pl.BlockSpec((tm, D_MODEL), lambda i: (i, 0)), _const_spec((1, D_MODEL))],
        out_specs=pl.BlockSpec((tm, D_MODEL), lambda i: (i, 0)),
        out_shape=jax.ShapeDtypeStruct((m, D_MODEL), out_dtype),
        compiler_params=_cparams(("parallel",)), name="rmsnorm",
    )(x, g.reshape(1, D_MODEL))


def _mm_kernel(x_ref, w_ref, o_ref):
    o_ref[...] = _dot(x_ref[...], w_ref[...])


def _inproj(xn, w):
    m = xn.shape[0]
    tm, tn = min(m, 1024), 1024
    return pl.pallas_call(
        _mm_kernel, grid=(N_U // tn, m // tm),
        in_specs=[pl.BlockSpec((tm, D_MODEL), lambda n, i: (i, 0)),
                  pl.BlockSpec((D_MODEL, tn), lambda n, i: (0, n))],
        out_specs=pl.BlockSpec((tm, tn), lambda n, i: (i, n)),
        out_shape=jax.ShapeDtypeStruct((m, N_U), F32),
        compiler_params=_cparams(("parallel", "parallel")), name="inproj",
    )(xn, w)


_INT_MIN = -2147483648


def _key_to_float(cs):
    bits = cs ^ ((cs >> 31) & jnp.int32(0x7FFFFFFF))
    return lax.bitcast_convert_type(bits, F32)


def _select_threshold(count_fn, keff, n_index_bits, full_index):
    zero = jnp.zeros_like(keff)

    def bit_body(bi, ukey):
        trial = ukey | lax.shift_left(jnp.int32(1), 31 - bi)
        cand = _key_to_float(trial ^ jnp.int32(_INT_MIN))
        cnt = count_fn(lambda s, idx: s >= cand)
        return jnp.where(cnt >= keff, trial, ukey)

    ukey = lax.fori_loop(0, 32, bit_body, zero)
    thr = _key_to_float(ukey ^ jnp.int32(_INT_MIN))
    c_gt = count_fn(lambda s, idx: s > thr)
    c_eq = count_fn(lambda s, idx: s == thr)
    need = keff - c_gt
    surplus = jnp.max(c_eq - need) > 0

    def tie_search():
        def body(bi, jm):
            trial = jm | lax.shift_left(jnp.int32(1), n_index_bits - 1 - bi)
            cnt = count_fn(lambda s, idx: (s == thr) & (idx < trial))
            return jnp.where(cnt < need, trial, jm)
        return lax.fori_loop(0, n_index_bits, body, zero)

    jm = lax.cond(surplus, tie_search, lambda: jnp.full_like(keff, full_index))
    return thr, jm


TQ = 128
TK = 512


def _dsa_prompt_kernel(q_ref, qi_ref, sm_ref, k_ref, v_ref, ki_ref, o_ref, kb, vb, kib, sc, *, t_len, topk):
    i = pl.program_id(1)

    @pl.when(i == 0)
    def _():
        def body(c, carry):
            r = pl.ds(pl.multiple_of(c * TK, TK), TK)
            kb[r, :] = _mx(k_ref[r, :])
            vb[r, :] = _mx(v_ref[r, :])
            kib[r, :] = _mx(ki_ref[r, S_IK:S_IK + IDX_DIM])
            return carry
        lax.fori_loop(0, t_len // TK, body, 0)

    nkt = (i * TQ) // TK + 1
    row = lax.broadcasted_iota(I32, (TQ, 1), 0)
    qpos = i * TQ + row
    lane_k = lax.broadcasted_iota(I32, (TQ, TK), 1)
    wi = sm_ref[:, S_IW:S_IW + IDX_HEADS]
    qib = _mx(qi_ref[...])

    def score_tile(j, carry):
        kt = kib[pl.ds(pl.multiple_of(j * TK, TK), TK), :]
        acc = jnp.zeros((TQ, TK), F32)
        for h in range(IDX_HEADS):
            s = _dot_nt(qib[:, h * IDX_DIM:(h + 1) * IDX_DIM], kt) * (IDX_DIM ** -0.5)
            acc = acc + jnp.maximum(s, 0.0) * wi[:, h:h + 1]
        score = acc * (IDX_HEADS ** -0.5)
        sc[j] = jnp.where(j * TK + lane_k <= qpos, score, NEG_INF)
        return carry

    lax.fori_loop(0, nkt, score_tile, 0)

    lane_g = lax.broadcasted_iota(I32, (TQ, LANES), 1)

    def count_fn(pred):
        def body(j, acc):
            t = sc[j]
            for g in range(TK // LANES):
                idx = j * TK + g * LANES + lane_g
                acc = acc + pred(t[:, g * LANES:(g + 1) * LANES], idx).astype(I32)
            return acc
        acc = lax.fori_loop(0, nkt, body, jnp.zeros((TQ, LANES), I32))
        return jnp.sum(acc, axis=1, keepdims=True)

    keff = jnp.minimum(jnp.int32(topk), qpos + 1)
    thr, jm = _select_threshold(count_fn, keff, max(1, (t_len - 1).bit_length()), t_len)

    def bias_tile(j, carry):
        t = sc[j]
        idx = j * TK + lane_k
        sel = (t > thr) | ((t == thr) & (idx <= jm))
        sc[j] = jnp.where(sel, 0.0, NEG_INF)
        return carry

    lax.fori_loop(0, nkt, bias_tile, 0)

    qb = _mx(q_ref[...])
    for h in range(HEADS):
        hs = slice(h * HD, (h + 1) * HD)
        qh = qb[:, hs]

        def body(j, carry, hs=hs, qh=qh):
            m, l, acc = carry
            r = pl.ds(pl.multiple_of(j * TK, TK), TK)
            lg = _dot_nt(qh, kb[r, hs]) * (HD ** -0.5) + sc[j]
            m_new = jnp.maximum(m, jnp.max(lg, axis=1, keepdims=True))
            m_safe = jnp.where(m_new == NEG_INF, 0.0, m_new)
            alpha = jnp.exp(m - m_safe)
            p = jnp.exp(lg - m_safe)
            l = alpha * l + jnp.sum(p, axis=1, keepdims=True)
            acc = alpha * acc + _dot(_mx(p), vb[r, hs])
            return m_new, l, acc

        init = (jnp.full((TQ, 1), NEG_INF, F32), jnp.zeros((TQ, 1), F32), jnp.zeros((TQ, HD), F32))
        _, l, acc = lax.fori_loop(0, nkt, body, init)
        o_ref[:, hs] = acc / l


def _dsa_prompt(u, bsz, t_len):
    nq = t_len // TQ
    topk = min(TOPK_MAX, t_len // 4)
    kern = functools.partial(_dsa_prompt_kernel, t_len=t_len, topk=topk)
    cb = lambda name: COL[name] // GROUP_W
    return pl.pallas_call(
        kern, grid=(bsz, nq),
        in_specs=[pl.BlockSpec((TQ, GROUP_W), lambda b, i: (b * nq + i, cb('a_q'))),
                  pl.BlockSpec((TQ, GROUP_W), lambda b, i: (b * nq + i, cb('a_iq'))),
                  pl.BlockSpec((TQ, LANES), lambda b, i: (b * nq + i, SMALL0 // LANES)),
                  _single((t_len, GROUP_W), lambda b, i: (b, cb('a_k'))),
                  _single((t_len, GROUP_W), lambda b, i: (b, cb('a_v'))),
                  _single((t_len, LANES), lambda b, i: (b, SMALL0 // LANES))],
        out_specs=pl.BlockSpec((TQ, GROUP_W), lambda b, i: (b * nq + i, 0)),
        out_shape=jax.ShapeDtypeStruct((bsz * t_len, GROUP_W), F32),
        scratch_shapes=[pltpu.VMEM((t_len, GROUP_W), MXU_DT), pltpu.VMEM((t_len, GROUP_W), MXU_DT),
                        pltpu.VMEM((t_len, IDX_DIM), MXU_DT), pltpu.VMEM((t_len // TK, TQ, TK), F32)],
        compiler_params=_cparams(("arbitrary", "arbitrary")), name="dsa_prompt",
    )(u, u, u, u, u, u)


_GLA_LEVELS = (64, 32, 16, 8, 4, 2, 1)


def _gla_constants():
    t = np.arange(CHUNK)
    mats = [(t[None, :] <= t[:, None])]
    masks = [np.eye(CHUNK, dtype=bool)]
    for hs in _GLA_LEVELS:
        bnd = (t // (2 * hs)) * 2 * hs + hs - 1
        mats.append(t[None, :] <= bnd[:, None])
        same = (t[:, None] // (2 * hs)) == (t[None, :] // (2 * hs))
        upper_q = ((t // hs) % 2 == 1)[:, None]
        lower_k = ((t // hs) % 2 == 0)[None, :]
        masks.append(same & upper_q & lower_k)
    mats.append(np.ones((CHUNK, CHUNK), bool))
    return (np.concatenate(mats, 0).astype(np.float32), np.stack(masks, 0).astype(np.float32))


def _mixers_kernel(rqk_ref, rv_ref, mqk_ref, mv_ref, gqk_ref, gv_ref, sm_ref, smt_ref,
                   cos_ref, sin_ref, dmat_ref, inter_ref, kdec_ref, gc_ref,
                   convw_ref, convb_ref, gbrow_ref, gbcol_ref, wg2_ref, bg2_ref, cs_ref, pm_ref, triu_ref,
                   ro_ref, mo_ref, go_ref, rs_out, mc_out, mn_out, mm_out, gs_out,
                   rs, mc, mn, mm, gs, cbuf):
    c = pl.program_id(1)

    @pl.when(c == 0)
    def _():
        rs[...] = jnp.zeros(rs.shape, F32)
        mc[...] = jnp.zeros(mc.shape, F32)
        mn[...] = jnp.zeros(mn.shape, F32)
        mm[...] = jnp.zeros(mm.shape, F32)
        gs[...] = jnp.zeros(gs.shape, F32)
        cbuf[0:8, :] = jnp.zeros((8, GROUP_W), F32)

    lane = lax.broadcasted_iota(I32, (CHUNK, LANES), 1)

    cosv, sinv = cos_ref[...], sin_ref[...]

    def rope(x):
        swapped = jnp.where((lane & 32) == 0, pltpu.roll(x, 96, 1), pltpu.roll(x, 32, 1))
        return x * cosv + swapped * sinv

    rqk = rqk_ref[...]
    rq = [rope(rqk[:, 0:128]) * (DK ** -0.5), rope(rqk[:, 128:256]) * (DK ** -0.5)]
    rk = [rope(rqk[:, 256:384]), rope(rqk[:, 384:512])]
    rv = rv_ref[...]
    for h in range(HEADS):
        ls = slice((h % 2) * DK, (h % 2 + 1) * DK)
        hs = slice(h * HD, (h + 1) * HD)
        qh, kh, vh = _mx(rq[h // 2][:, ls]), rk[h // 2][:, ls], _mx(rv[:, hs])
        att = _dot_nt(qh, _mx(kh)) * dmat_ref[h]
        s_old = rs[h]
        ro_ref[:, hs] = _dot(_mx(att), vh) + _dot(qh, _mx(s_old)) * inter_ref[h]
        rs[h] = s_old * gc_ref[h, 0:1, :] + _dot_tn(_mx(kh * kdec_ref[h]), vh)

    cbuf[8:8 + CHUNK, :] = mqk_ref[...]
    y = convb_ref[...]
    for j in range(CONV_W):
        y = y + cbuf[pl.ds(8 - (CONV_W - 1) + j, CHUNK), :] * convw_ref[j:j + 1, :]
    cbuf[0:8, :] = cbuf[CHUNK:CHUNK + 8, :]
    y = y * jax.nn.sigmoid(y)
    smp = sm_ref[...] + gbrow_ref[...]
    smtp = smt_ref[...] + gbcol_ref[...]
    bcol_all = _dot(cs_ref[0:CHUNK, :], jax.nn.log_sigmoid(smp), precision=HIGHEST)
    brow_all = _dot(jax.nn.log_sigmoid(smtp), triu_ref[...], precision=HIGHEST)
    tri = cs_ref[0:CHUNK, :] > 0.5
    mv = mv_ref[...]
    for h in range(HEADS):
        hs = slice(h * HD, (h + 1) * HD)
        qh = _mx(y[:, h * DK:(h + 1) * DK])
        kh = y[:, 256 + h * DK:256 + (h + 1) * DK] * (DK ** -0.5)
        vh = mv[:, hs]
        b_col = bcol_all[:, S_MF + h:S_MF + h + 1]
        i_col = smp[:, S_MI + h:S_MI + h + 1]
        b_row = brow_all[S_MF + h:S_MF + h + 1, :]
        i_row = smtp[S_MI + h:S_MI + h + 1, :]
        m_prev = mm[h][0:1, 0:1]
        dlog = jnp.where(tri, b_col - b_row + i_row, NEG_INF)
        m_inter = b_col + m_prev
        m_t = jnp.maximum(m_inter, jnp.max(dlog, axis=1, keepdims=True))
        wts = jnp.exp(dlog - m_t)
        scale = jnp.exp(m_inter - m_t)
        qk = _dot_nt(qh, _mx(kh)) * wts
        c_old, n_old = mc[h], mn[h][0:1, :]
        num = _dot(_mx(qk), _mx(vh)) + scale * _dot_nt(qh, _mx(c_old))
        qf = y[:, h * DK:(h + 1) * DK]
        dsum = jnp.sum(qk, axis=1, keepdims=True) + scale * jnp.sum(qf * n_old, axis=1, keepdims=True)
        den = jnp.maximum(jnp.abs(dsum), jnp.exp(-m_t))
        mo_ref[:, hs] = num / den
        m_new = m_t[CHUNK - 1:CHUNK, :]
        s_last = scale[CHUNK - 1:CHUNK, :]
        b_last = b_col[CHUNK - 1:CHUNK, :]
        w_last = jnp.exp(b_last - b_col + i_col - m_new)
        mc[h] = s_last * c_old + _dot_tn(_mx(vh * w_last), _mx(kh))
        mn[h] = jnp.broadcast_to(s_last * n_old + jnp.sum(kh * w_last, axis=0, keepdims=True), (8, DK))
        mm[h] = jnp.broadcast_to(m_new, (8, LANES))

    glr = sm_ref[:, S_LR:S_LR + LOWRANK]
    la = jax.nn.log_sigmoid(_dot(_mx(glr), _mx(wg2_ref[...])) + bg2_ref[...]) / G_TAU
    ball = _dot(cs_ref[...], la, precision=HIGHEST)
    nlev = len(_GLA_LEVELS)
    b = ball[0:CHUNK]
    b_tot = ball[(nlev + 1) * CHUNK:(nlev + 2) * CHUNK]
    gqk = gqk_ref[...]
    gq = gqk[:, 0:256] * (DK ** -0.5)
    gk = gqk[:, 256:512]
    qf_l, kf_l = [gq], [gk]
    for lv in range(nlev):
        bnd = ball[(lv + 1) * CHUNK:(lv + 2) * CHUNK]
        qf_l.append(gq * jnp.exp(jnp.minimum(b - bnd, 0.0)))
        kf_l.append(gk * jnp.exp(jnp.minimum(bnd - b, 0.0)))
    qe = gq * jnp.exp(b)
    kl = gk * jnp.exp(b_tot - b)
    eb = jnp.exp(b_tot[0:1, :])
    gv = gv_ref[...]
    for h in range(HEADS):
        ds_ = slice(h * DK, (h + 1) * DK)
        hs = slice(h * HD, (h + 1) * HD)
        att = jnp.zeros((CHUNK, CHUNK), F32)
        for lv in range(nlev + 1):
            att = att + pm_ref[lv] * _dot_nt(_mx(qf_l[lv][:, ds_]), _mx(kf_l[lv][:, ds_]))
        vh = _mx(gv[:, hs])
        st_old = gs[h]
        go_ref[:, hs] = _dot(_mx(att), vh) + _dot_nt(_mx(qe[:, ds_]), _mx(st_old))
        gs[h] = st_old * eb[:, ds_] + _dot_tn(vh, _mx(kl[:, ds_]))

    @pl.when(c == pl.num_programs(1) - 1)
    def _():
        rs_out[...] = rs[...]
        mc_out[...] = mc[...]
        mn_out[...] = mn[...]
        mm_out[...] = mm[...]
        gs_out[...] = gs[...]


def _retention_tables(chunk):
    log_g = jnp.log1p(-jnp.exp2(-5.0 - jnp.arange(HEADS, dtype=F32)))
    idx = jnp.arange(chunk, dtype=F32)
    diff = idx[:, None] - idx[None, :]
    causal = diff >= 0
    dmat = jnp.where(causal[None], jnp.exp(jnp.where(causal, diff, 0.0)[None] * log_g[:, None, None]), 0.0)
    inter = jnp.exp((idx + 1.0)[None, :] * log_g[:, None])
    kdec = jnp.exp((chunk - 1.0 - idx)[None, :] * log_g[:, None])
    g_c = jnp.exp(chunk * log_g)
    return dmat, inter, kdec, g_c


def _rope_tables(pos):
    half = DK // 2
    inv = jnp.exp(-math.log(ROPE_BASE) * jnp.arange(half, dtype=F32) * (2.0 / DK))
    ang = pos.astype(F32)[:, None] * inv[None, :]
    return jnp.cos(ang), jnp.sin(ang)


def _mixers_prompt(u, smt, bsz, t_len, lw):
    nc = t_len // CHUNK
    cos, sin = _rope_tables(jnp.arange(t_len))
    cosf = jnp.tile(jnp.concatenate([cos, cos], -1), (1, 2))
    sinf = jnp.tile(jnp.concatenate([-sin, sin], -1), (1, 2))
    dmat, inter, kdec, g_c = _retention_tables(CHUNK)
    inter_b = jnp.broadcast_to(inter[:, :, None], (HEADS, CHUNK, HD))
    kdec_b = jnp.broadcast_to(kdec[:, :, None], (HEADS, CHUNK, DK))
    gc_b = jnp.broadcast_to(g_c[:, None, None], (HEADS, 8, HD))
    cs_np, pm_np = _gla_constants()
    cs, pm = jnp.asarray(cs_np), jnp.asarray(pm_np)
    triu = jnp.asarray(cs_np[0:CHUNK].T.copy())
    gb = jnp.zeros((LANES,), F32).at[S_MI:S_MI + HEADS].set(lw['i_b']).at[S_MF:S_MF + HEADS].set(lw['f_b'])
    rowblk = lambda name: (lambda b, c, n=COL[name] // GROUP_W: (b * nc + c, n))
    cspec = lambda shape: _const_spec(shape)
    m = bsz * t_len
    st = lambda shape: pl.BlockSpec((None,) + shape, lambda b, c: (b,) + (0,) * len(shape))
    outs = pl.pallas_call(
        _mixers_kernel, grid=(bsz, nc),
        in_specs=[pl.BlockSpec((CHUNK, GROUP_W), rowblk('r_q')), pl.BlockSpec((CHUNK, GROUP_W), rowblk('r_v')),
                  pl.BlockSpec((CHUNK, GROUP_W), rowblk('m_q')), pl.BlockSpec((CHUNK, GROUP_W), rowblk('m_v')),
                  pl.BlockSpec((CHUNK, GROUP_W), rowblk('g_q')), pl.BlockSpec((CHUNK, GROUP_W), rowblk('g_v')),
                  pl.BlockSpec((CHUNK, LANES), lambda b, c: (b * nc + c, SMALL0 // LANES)),
                  pl.BlockSpec((LANES, CHUNK), lambda b, c: (0, b * nc + c)),
                  pl.BlockSpec((CHUNK, LANES), lambda b, c: (c, 0)), pl.BlockSpec((CHUNK, LANES), lambda b, c: (c, 0)),
                  cspec((HEADS, CHUNK, CHUNK)), cspec((HEADS, CHUNK, HD)), cspec((HEADS, CHUNK, DK)),
                  cspec((HEADS, 8, HD)),
                  cspec((CONV_W, GROUP_W)), cspec((1, GROUP_W)), cspec((1, LANES)), cspec((LANES, 1)),
                  cspec((LOWRANK, 256)), cspec((1, 256)), cspec(cs.shape), cspec(pm.shape), cspec((CHUNK, CHUNK))],
        out_specs=[pl.BlockSpec((CHUNK, GROUP_W), lambda b, c: (b * nc + c, 0))] * 3
        + [st((HEADS, DK, HD)), st((HEADS, HD, DK)), st((HEADS, 8, DK)), st((HEADS, 8, LANES)), st((HEADS, HD, DK))],
        out_shape=[jax.ShapeDtypeStruct((m, GROUP_W), F32)] * 3
        + [jax.ShapeDtypeStruct((bsz, HEADS, DK, HD), F32), jax.ShapeDtypeStruct((bsz, HEADS, HD, DK), F32),
           jax.ShapeDtypeStruct((bsz, HEADS, 8, DK), F32), jax.ShapeDtypeStruct((bsz, HEADS, 8, LANES), F32),
           jax.ShapeDtypeStruct((bsz, HEADS, HD, DK), F32)],
        scratch_shapes=[pltpu.VMEM((HEADS, DK, HD), F32), pltpu.VMEM((HEADS, HD, DK), F32),
                        pltpu.VMEM((HEADS, 8, DK), F32), pltpu.VMEM((HEADS, 8, LANES), F32),
                        pltpu.VMEM((HEADS, HD, DK), F32), pltpu.VMEM((CHUNK + 8, GROUP_W), F32)],
        compiler_params=_cparams(("arbitrary", "arbitrary")), name="mixers_prompt",
    )(u, u, u, u, u, u, u, smt, cosf, sinf, dmat, inter_b, kdec_b, gc_b,
      lw['conv_w'], lw['conv_b'].reshape(1, GROUP_W), gb.reshape(1, LANES), gb.reshape(LANES, 1),
      lw['wg2'], lw['bg2'].reshape(1, 256), cs, pm, triu)
    r_o, m_o, g_o, rs, mc, mn, mm, gst = outs
    return r_o, m_o, g_o, rs, mc, mn[:, :, 0, :], mm[:, :, 0, 0], jnp.swapaxes(gst, -1, -2)


def _head_norm(x, center):
    outs = []
    for h in range(HEADS):
        xh = x[:, h * HD:(h + 1) * HD]
        if center:
            xh = xh - jnp.mean(xh, axis=-1, keepdims=True)
        outs.append(xh * lax.rsqrt(jnp.mean(xh * xh, axis=-1, keepdims=True) + EPS))
    return jnp.concatenate(outs, axis=-1)


def _silu(x):
    return x * jax.nn.sigmoid(x)


def _out_kernel(ao_ref, ro_ref, mo_ref, go_ref, az_ref, rz_ref, mg_ref, mz_ref, gz_ref, h_ref, p_ref,
                rg_ref, mgn_ref, gg_ref, wout_ref, plew_ref, plegw_ref, gn_ref, hn_ref, xn_ref):
    a_mix = ao_ref[...] * _silu(az_ref[...])
    r_mix = _head_norm(ro_ref[...], True) * rg_ref[...] * _silu(rz_ref[...])
    m_h = mo_ref[...] * jax.nn.sigmoid(mg_ref[...])
    m_mix = _head_norm(m_h, True) * mgn_ref[...] * _silu(mz_ref[...])
    g_mix = _head_norm(go_ref[...], False) * gg_ref[...] * _silu(gz_ref[...])
    h1 = h_ref[...]
    for g, mix in enumerate((a_mix, r_mix, m_mix, g_mix)):
        h1 = h1 + _dot(_mx(mix), wout_ref[g * GROUP_W:(g + 1) * GROUP_W, :])
    gate = jax.nn.sigmoid(_dot(_mx(h1), plegw_ref[...]))
    h2 = h1 + gate * _dot(_mx(p_ref[...]), plew_ref[...])
    hn_ref[...] = h2
    y = h2 * lax.rsqrt(jnp.mean(h2 * h2, axis=-1, keepdims=True) + EPS)
    xn_ref[...] = (y * gn_ref[...]).astype(xn_ref.dtype)


def _out_stage(a_o, r_o, m_o, g_o, u, h, p, lw, g_next, next_dtype):
    m = h.shape[0]
    tm = min(m, 256)
    rowb = lambda i: (i, 0)
    ub = lambda name: (lambda i, n=COL[name] // GROUP_W: (i, n))
    tile = lambda w: pl.BlockSpec((tm, w), rowb)
    return pl.pallas_call(
        _out_kernel, grid=(m // tm,),
        in_specs=[tile(GROUP_W)] * 4
        + [pl.BlockSpec((tm, GROUP_W), ub(n)) for n in ('a_z', 'r_z', 'm_o', 'm_z', 'g_z')]
        + [tile(D_MODEL), tile(PLE_DIM), _const_spec((1, GROUP_W)), _const_spec((1, GROUP_W)), _const_spec((1, GROUP_W)),
           _single((D_MODEL, D_MODEL), lambda i: (0, 0)), _single((PLE_DIM, D_MODEL), lambda i: (0, 0)),
           _single((D_MODEL, D_MODEL), lambda i: (0, 0)), _const_spec((1, D_MODEL))],
        out_specs=[tile(D_MODEL), tile(D_MODEL)],
        out_shape=[jax.ShapeDtypeStruct((m, D_MODEL), F32), jax.ShapeDtypeStruct((m, D_MODEL), next_dtype)],
        compiler_params=_cparams(("parallel",)), name="out_stage",
    )(a_o, r_o, m_o, g_o, u, u, u, u, u, h, p,
      lw['rg'].reshape(1, GROUP_W), lw['mg'].reshape(1, GROUP_W), lw['gg'].reshape(1, GROUP_W),
      lw['w_out'], lw['ple_w'], lw['ple_gw'], g_next.reshape(1, D_MODEL))


def _sample_scores_kernel(pt_ref, qi_ref, ust_ref, *refs, bsz):
    pages, o_ref = refs[:bsz], refs[bsz]
    j = pl.program_id(0)
    for b in range(bsz):
        qrow = qi_ref[b:b + 1, :]
        qh = _mx(jnp.concatenate([qrow[:, h * IDX_DIM:(h + 1) * IDX_DIM] for h in range(IDX_HEADS)], axis=0))
        s = _dot_nt(qh, _mx(pages[b][...])) * (IDX_DIM ** -0.5)
        w_col = ust_ref[S_IW:S_IW + IDX_HEADS, b:b + 1]
        score = jnp.sum(jnp.maximum(s, 0.0) * w_col, axis=0, keepdims=True) * (IDX_HEADS ** -0.5)
        o_ref[b, pl.ds(j, 1), :] = score


def _sample_scores(cache_kidx, page_table, u_s, ust_small, layer):
    bsz, n_pages = page_table.shape
    kern = functools.partial(_sample_scores_kernel, bsz=bsz)
    page_spec = lambda b: pl.BlockSpec((None, None, PAGE, IDX_DIM), lambda j, pt, b=b: (layer, pt[b, j], 0, 0))
    return pl.pallas_call(
        kern,
        grid_spec=pltpu.PrefetchScalarGridSpec(
            num_scalar_prefetch=1, grid=(n_pages,),
            in_specs=[pl.BlockSpec((bsz, GROUP_W), lambda j, pt: (0, COL['a_iq'] // GROUP_W)),
                      pl.BlockSpec((LANES, bsz), lambda j, pt: (0, 0))]
            + [page_spec(b) for b in range(bsz)],
            out_specs=pl.BlockSpec((bsz, n_pages, PAGE), lambda j, pt: (0, 0, 0))),
        out_shape=jax.ShapeDtypeStruct((bsz, n_pages, PAGE), F32),
        compiler_params=_cparams(("arbitrary",)), name="sample_scores",
    )(page_table, u_s, ust_small, *([cache_kidx] * bsz))


def _sample_select_kernel(sc_ref, qi_ref, sm_ref, bias_ref, nbias_ref, *, n_pages, topk):
    sc = sc_ref[...]
    bsz = sc.shape[0]
    past = n_pages * PAGE
    qi, sm = qi_ref[...], sm_ref[...]
    ki = sm[:, :, S_IK:S_IK + IDX_DIM]
    acc = jnp.zeros((bsz, 1, 1), F32)
    for h in range(IDX_HEADS):
        s = jnp.sum(qi[:, :, h * IDX_DIM:(h + 1) * IDX_DIM] * ki, axis=2, keepdims=True) * (IDX_DIM ** -0.5)
        acc = acc + jnp.maximum(s, 0.0) * sm[:, :, S_IW + h:S_IW + h + 1]
    s_new = acc * (IDX_HEADS ** -0.5)
    idx = (lax.broadcasted_iota(I32, sc.shape, 1) * PAGE + lax.broadcasted_iota(I32, sc.shape, 2))

    def count_fn(pred):
        c = jnp.sum(pred(sc, idx).astype(I32), axis=2, keepdims=True)
        c = jnp.sum(c, axis=1, keepdims=True)
        return c + pred(s_new, jnp.int32(past)).astype(I32)

    keff = jnp.full((bsz, 1, 1), min(topk, past + 1), I32)
    thr, jm = _select_threshold(count_fn, keff, max(1, past.bit_length()), past + 1)
    sel = (sc > thr) | ((sc == thr) & (idx <= jm))
    bias_ref[...] = jnp.where(sel, 0.0, NEG_INF)
    sel_new = (s_new > thr) | ((s_new == thr) & (jnp.int32(past) <= jm))
    nbias_ref[...] = jnp.broadcast_to(jnp.where(sel_new, 0.0, NEG_INF), nbias_ref.shape)


def _sample_select(scores, u_s, topk):
    bsz, n_pages, _ = scores.shape
    kern = functools.partial(_sample_select_kernel, n_pages=n_pages, topk=topk)
    return pl.pallas_call(
        kern, grid=(1,),
        in_specs=[_const_spec(scores.shape),
                  pl.BlockSpec((bsz, 1, GROUP_W), lambda i: (0, 0, COL['a_iq'] // GROUP_W)),
                  pl.BlockSpec((bsz, 1, LANES), lambda i: (0, 0, SMALL0 // LANES))],
        out_specs=[_const_spec(scores.shape), _const_spec((bsz, 1, LANES))],
        out_shape=[jax.ShapeDtypeStruct(scores.shape, F32), jax.ShapeDtypeStruct((bsz, 1, LANES), F32)],
        compiler_params=_cparams(("arbitrary",)), name="sample_select",
    )(scores, u_s.reshape(bsz, 1, N_U), u_s.reshape(bsz, 1, N_U))


def _flash_update(state, b, lg, vt):
    m_ref, l_ref, acc_ref = state
    m = m_ref[b]
    m_new = jnp.maximum(m, jnp.max(lg, axis=1, keepdims=True))
    m_safe = jnp.where(m_new == NEG_INF, 0.0, m_new)
    alpha = jnp.exp(m - m_safe)
    p = jnp.exp(lg - m_safe)
    l_ref[b] = alpha * l_ref[b] + jnp.sum(p, axis=1, keepdims=True)
    acc_ref[b] = alpha * acc_ref[b] + _dot(_mx(p), _mx(vt))
    m_ref[b] = m_new


def _sample_attn_kernel(pt_ref, q_ref, knew_ref, vnew_ref, bias_ref, nbias_ref, *refs, bsz):
    kp, vp = refs[:bsz], refs[bsz:2 * bsz]
    o_ref = refs[2 * bsz]
    m_s, l_s, acc_s = refs[2 * bsz + 1:]
    state = (m_s, l_s, acc_s)
    j = pl.program_id(0)
    hrow = lax.broadcasted_iota(I32, (8, GROUP_W), 0)
    hlane = lax.broadcasted_iota(I32, (8, GROUP_W), 1) // HD

    def qmat(b):
        return _mx(jnp.where(hrow == hlane, jnp.broadcast_to(q_ref[b:b + 1, :], (8, GROUP_W)), 0.0))

    @pl.when(j == 0)
    def _():
        m_s[...] = jnp.full(m_s.shape, NEG_INF, F32)
        l_s[...] = jnp.zeros(l_s.shape, F32)
        acc_s[...] = jnp.zeros(acc_s.shape, F32)
        key_id = lax.broadcasted_iota(I32, (8, bsz), 1)
        for b in range(bsz):
            lg = _dot_nt(qmat(b), _mx(knew_ref[...])) * (HD ** -0.5)
            lg = jnp.where(key_id == b, lg + nbias_ref[b][:, 0:bsz], NEG_INF)
            _flash_update(state, b, lg, vnew_ref[...])

    for b in range(bsz):
        lg = _dot_nt(qmat(b), _mx(kp[b][...])) * (HD ** -0.5) + bias_ref[b, pl.ds(j, 1), :]
        _flash_update(state, b, lg, vp[b][...])

    @pl.when(j == pl.num_programs(0) - 1)
    def _():
        rows = []
        for b in range(bsz):
            o = acc_s[b] / l_s[b]
            rows.append(jnp.sum(jnp.where(hrow == hlane, o, 0.0), axis=0, keepdims=True))
        o_ref[...] = jnp.concatenate(rows, axis=0)


def _sample_attn(cache_k, cache_v, page_table, u_s, bias, nbias, layer):
    bsz, n_pages = page_table.shape
    kern = functools.partial(_sample_attn_kernel, bsz=bsz)
    page_spec = lambda b: pl.BlockSpec((None, None, PAGE, GROUP_W), lambda j, pt, b=b: (layer, pt[b, j], 0, 0))
    ucol = lambda name: pl.BlockSpec((bsz, GROUP_W), lambda j, pt, n=COL[name] // GROUP_W: (0, n))
    return pl.pallas_call(
        kern,
        grid_spec=pltpu.PrefetchScalarGridSpec(
            num_scalar_prefetch=1, grid=(n_pages,),
            in_specs=[ucol('a_q'), ucol('a_k'), ucol('a_v'),
                      pl.BlockSpec(bias.shape, lambda j, pt: (0, 0, 0)),
                      pl.BlockSpec(nbias.shape, lambda j, pt: (0, 0, 0))]
            + [page_spec(b) for b in range(bsz)] * 2,
            out_specs=pl.BlockSpec((bsz, GROUP_W), lambda j, pt: (0, 0)),
            scratch_shapes=[pltpu.VMEM((bsz, 8, 1), F32), pltpu.VMEM((bsz, 8, 1), F32),
                            pltpu.VMEM((bsz, 8, GROUP_W), F32)]),
        out_shape=jax.ShapeDtypeStruct((bsz, GROUP_W), F32),
        compiler_params=_cparams(("arbitrary",)), name="sample_attn",
    )(page_table, u_s, u_s, u_s, bias, nbias, *([cache_k] * bsz), *([cache_v] * bsz))


def _decode_kernel(us_ref, ust_ref, rs_ref, mc_ref, mct_ref, mn_ref, mm_ref, cb_ref, cbt_ref, gs_ref,
                   cosr_ref, sinr_ref, cosc_ref, sinc_ref, gam_ref,
                   convw_ref, convb_ref, convwt_ref, convbt_ref, ib_ref, fb_ref, wg2_ref, bg2_ref, wg2t_ref, bg2t_ref,
                   ro_ref, mo_ref, go_ref, rs_out, mc_out, mn_out, mm_out, gs_out, *, bsz):
    us = lambda name, w: us_ref[:, COL[name]:COL[name] + w]
    ut = lambda name, w: ust_ref[COL[name]:COL[name] + w, :]
    lane = lax.broadcasted_iota(I32, (bsz, LANES), 1)

    def rope_row(x):
        swapped = jnp.where((lane & 32) == 0, pltpu.roll(x, 96, 1), pltpu.roll(x, 32, 1))
        return x * cosr_ref[...] + swapped * sinr_ref[...]

    def rope_col(x):
        x1, x2 = x[0:32, :], x[32:64, :]
        cc, ss = cosc_ref[...], sinc_ref[...]
        return jnp.concatenate([x1 * cc - x2 * ss, x1 * ss + x2 * cc], axis=0)

    rq, rk = us('r_q', 256), us('r_k', 256)
    rq_row = jnp.concatenate([rope_row(rq[:, 0:128]), rope_row(rq[:, 128:256])], axis=1) * (DK ** -0.5)
    rk_row = jnp.concatenate([rope_row(rk[:, 0:128]), rope_row(rk[:, 128:256])], axis=1)
    rv = us('r_v', 512)
    rqt, rkt = ut('r_q', 256), ut('r_k', 256)
    for h in range(HEADS):
        ds_ = slice(h * DK, (h + 1) * DK)
        hs = slice(h * HD, (h + 1) * HD)
        gam = gam_ref[h, 0:1, :]
        qk = jnp.sum(rq_row[:, ds_] * rk_row[:, ds_], axis=1, keepdims=True)
        intra = qk * rv[:, hs]
        q_col = rope_col(rqt[ds_, :]) * (DK ** -0.5)
        k_col = rope_col(rkt[ds_, :])
        rows = []
        for b in range(bsz):
            s_old = rs_ref[b, h]
            rows.append(jnp.sum(q_col[:, b:b + 1] * s_old, axis=0, keepdims=True))
            rs_out[b, h] = s_old * gam + k_col[:, b:b + 1] * rv[b:b + 1, hs]
        ro_ref[:, hs] = intra + jnp.concatenate(rows, axis=0) * gam

    y = convb_ref[...]
    yt = convbt_ref[...]
    for j in range(CONV_W - 1):
        y = y + cb_ref[j] * convw_ref[j:j + 1, :]
        yt = yt + cbt_ref[j] * convwt_ref[:, j:j + 1]
    y = y + us('m_q', 512) * convw_ref[CONV_W - 1:CONV_W, :]
    yt = yt + ut('m_q', 512) * convwt_ref[:, CONV_W - 1:CONV_W]
    y = y * jax.nn.sigmoid(y)
    yt = yt * jax.nn.sigmoid(yt)
    mv = us('m_v', 512)
    mvt = ut('m_v', 512)
    i_all = us('m_i', HEADS) + ib_ref[...]
    lf_all = jax.nn.log_sigmoid(us('m_f', HEADS) + fb_ref[...])
    for h in range(HEADS):
        ds_ = slice(h * DK, (h + 1) * DK)
        hs = slice(h * HD, (h + 1) * HD)
        q_row = y[:, ds_]
        k_row = y[:, 256 + h * DK:256 + (h + 1) * DK] * (DK ** -0.5)
        q_col = yt[ds_, :]
        i_h, lf_h, m0 = i_all[:, h:h + 1], lf_all[:, h:h + 1], mm_ref[:, h:h + 1]
        m_inter = lf_h + m0
        m_t = jnp.maximum(m_inter, i_h)
        wts = jnp.exp(i_h - m_t)
        scale = jnp.exp(m_inter - m_t)
        qk = jnp.sum(q_row * k_row, axis=1, keepdims=True) * wts
        n_old = mn_ref[:, ds_]
        rows = []
        scale_b = jnp.broadcast_to(scale, (bsz, DK))
        kw = k_row * wts
        for b in range(bsz):
            rows.append(jnp.sum(q_col[:, b:b + 1] * mct_ref[b, h], axis=0, keepdims=True))
            mc_out[b, h] = mc_ref[b, h] * scale_b[b:b + 1, :] + mvt[hs, b:b + 1] * kw[b:b + 1, :]
        num = qk * mv[:, hs] + scale * jnp.concatenate(rows, axis=0)
        dsum = qk + scale * jnp.sum(n_old * q_row, axis=1, keepdims=True)
        den = jnp.maximum(jnp.abs(dsum), jnp.exp(-m_t))
        mo_ref[:, hs] = num / den
        mn_out[:, ds_] = scale * n_old + kw
        mm_out[:, h:h + 1] = m_t

    la_col = jax.nn.log_sigmoid(_dot(_mx(wg2t_ref[...]), _mx(ut('g_lr', LOWRANK))) + bg2t_ref[...]) / G_TAU
    gq_row, gk_row = us('g_q', 256) * (DK ** -0.5), us('g_k', 256)
    gqt, gkt = ut('g_q', 256) * (DK ** -0.5), ut('g_k', 256)
    gv = us('g_v', 512)
    ea_col = jnp.exp(la_col)
    for h in range(HEADS):
        ds_ = slice(h * DK, (h + 1) * DK)
        hs = slice(h * HD, (h + 1) * HD)
        qk = jnp.sum(gq_row[:, ds_] * gk_row[:, ds_], axis=1, keepdims=True)
        qe_col = gqt[ds_, :] * ea_col[ds_, :]
        rows = []
        for b in range(bsz):
            s_old = gs_ref[b, h]
            rows.append(jnp.sum(qe_col[:, b:b + 1] * s_old, axis=0, keepdims=True))
            gs_out[b, h] = ea_col[ds_, b:b + 1] * s_old + gkt[ds_, b:b + 1] * gv[b:b + 1, hs]
        go_ref[:, hs] = qk * gv[:, hs] + jnp.concatenate(rows, axis=0)


def _decode_step(u_s, ust, st, lw, past):
    bsz = u_s.shape[0]
    ret_s, m_c, m_n, m_m, conv_buf, gla_s = st
    cos, sin = _rope_tables(past + jnp.arange(1))
    cosr = jnp.tile(jnp.concatenate([cos, cos], -1), (1, 2))
    sinr = jnp.tile(jnp.concatenate([-sin, sin], -1), (1, 2))
    _, _, _, g_c = _retention_tables(1)
    gam = jnp.broadcast_to(g_c[:, None, None], (HEADS, 8, HD))
    full = lambda a: _const_spec(a.shape)
    args = [u_s, ust, ret_s, m_c, jnp.swapaxes(m_c, -1, -2), m_n.reshape(bsz, HEADS * DK), m_m,
            jnp.swapaxes(conv_buf, 0, 1), jnp.transpose(conv_buf, (1, 2, 0)), gla_s,
            cosr, sinr, cos.reshape(DK // 2, 1), sin.reshape(DK // 2, 1), gam,
            lw['conv_w'], lw['conv_b'].reshape(1, GROUP_W), lw['conv_w'].T, lw['conv_b'].reshape(GROUP_W, 1),
            lw['i_b'].reshape(1, HEADS), lw['f_b'].reshape(1, HEADS),
            lw['wg2'], lw['bg2'].reshape(1, 256), lw['wg2'].T, lw['bg2'].reshape(256, 1)]
    outs = pl.pallas_call(
        functools.partial(_decode_kernel, bsz=bsz), grid=(1,),
        in_specs=[full(a) for a in args],
        out_specs=[_const_spec((bsz, GROUP_W))] * 3
        + [full(ret_s), full(m_c), _const_spec((bsz, HEADS * DK)), full(m_m), full(gla_s)],
        out_shape=[jax.ShapeDtypeStruct((bsz, GROUP_W), F32)] * 3
        + [jax.ShapeDtypeStruct(ret_s.shape, F32), jax.ShapeDtypeStruct(m_c.shape, F32),
           jax.ShapeDtypeStruct((bsz, HEADS * DK), F32), jax.ShapeDtypeStruct(m_m.shape, F32),
           jax.ShapeDtypeStruct(gla_s.shape, F32)],
        compiler_params=_cparams(("arbitrary",)), name="decode_step",
    )(*args)
    r_o, m_o, g_o, rs, mc, mn, mm, gs = outs
    return r_o, m_o, g_o, rs, mc, mn.reshape(bsz, HEADS, DK), mm, gs


def _prep_w_in(w):
    cols = [w[:, _ORIG[n][0]:_ORIG[n][0] + _ORIG[n][1]] for n in _MAIN + _SMALL]
    cols.append(jnp.zeros((w.shape[0], N_U - _USED), w.dtype))
    return jnp.concatenate(cols, axis=1).astype(MXU_DT)


def kernel(x_prompt, x_sample, cache_k, cache_v, cache_kidx, page_table, state_ret, state_mlstm_c, state_mlstm_n,
           state_mlstm_m, state_mlstm_conv, state_gla, p_prompt, p_sample, norm_g, w_in, conv_w, conv_b,
           m_igate_b, m_fgate_b, gla_wg2, gla_bg2, ret_norm_g, mlstm_norm_g, gla_norm_g, w_out, ple_w,
           ple_gate_w, final_g):
    bp, tp, _ = x_prompt.shape
    bs, ts, _ = x_sample.shape
    assert ts == 1 and tp % TK == 0 and tp % CHUNK == 0
    n_pages = page_table.shape[1]
    past = n_pages * PAGE
    n_pool = cache_k.shape[1]
    topk_s = min(TOPK_MAX, (past + ts) // 4)
    cache_k2 = cache_k.reshape(DEPTH, n_pool, PAGE, GROUP_W)
    cache_v2 = cache_v.reshape(DEPTH, n_pool, PAGE, GROUP_W)
    mp, ms = bp * tp, bs * ts

    hp = x_prompt.reshape(mp, D_MODEL)
    hs = x_sample.reshape(ms, D_MODEL)
    xn_p = _rmsnorm(hp, norm_g[0], MXU_DT)
    xn_s = _rmsnorm(hs, norm_g[0], MXU_DT)
    outs_p, outs_s = [], []
    y_p = y_s = None
    for l in range(DEPTH):
        lw = dict(conv_w=conv_w[l], conv_b=conv_b[l], i_b=m_igate_b[l], f_b=m_fgate_b[l], wg2=gla_wg2[l],
                  bg2=gla_bg2[l], rg=ret_norm_g[l], mg=mlstm_norm_g[l], gg=gla_norm_g[l],
                  w_out=w_out[l].astype(MXU_DT), ple_w=ple_w[l].astype(MXU_DT), ple_gw=ple_gate_w[l].astype(MXU_DT))
        w_l = _prep_w_in(w_in[l])
        last = l == DEPTH - 1
        g_next = final_g if last else norm_g[l + 1]
        nd = F32 if last else MXU_DT

        u = _inproj(xn_p, w_l)
        smt = u[:, SMALL0:SMALL0 + LANES].T
        a_o = _dsa_prompt(u, bp, tp)
        r_o, m_o, g_o, rs, mc, mn, mm, gs = _mixers_prompt(u, smt, bp, tp, lw)
        hp, nxt = _out_stage(a_o, r_o, m_o, g_o, u, hp, p_prompt[l].reshape(mp, PLE_DIM), lw, g_next, nd)
        u3 = u.reshape(bp, tp, N_U)
        outs_p.append((u3[:, :, COL['a_k']:COL['a_k'] + GROUP_W].reshape(bp, tp, HEADS, HD),
                       u3[:, :, COL['a_v']:COL['a_v'] + GROUP_W].reshape(bp, tp, HEADS, HD),
                       u3[:, :, COL['a_ik']:COL['a_ik'] + IDX_DIM],
                       rs, mc, mn, mm, u3[:, tp - (CONV_W - 1):, COL['m_q']:COL['m_q'] + GROUP_W], gs))
        if last:
            y_p = nxt
        else:
            xn_p = nxt

        us = _inproj(xn_s, w_l)
        ust = us.T
        scores = _sample_scores(cache_kidx, page_table, us, ust[SMALL0:SMALL0 + LANES], l)
        bias, nbias = _sample_select(scores, us, topk_s)
        a_o = _sample_attn(cache_k2, cache_v2, page_table, us, bias, nbias, l)
        st = (state_ret[l], state_mlstm_c[l], state_mlstm_n[l], state_mlstm_m[l], state_mlstm_conv[l], state_gla[l])
        r_o, m_o, g_o, rs, mc, mn, mm, gs = _decode_step(us, ust, st, lw, past)
        hs, nxt = _out_stage(a_o, r_o, m_o, g_o, us, hs, p_sample[l].reshape(ms, PLE_DIM), lw, g_next, nd)
        conv_new = jnp.concatenate([state_mlstm_conv[l][:, 1:], us[:, None, COL['m_q']:COL['m_q'] + GROUP_W]], axis=1)
        outs_s.append((us[:, COL['a_k']:COL['a_k'] + GROUP_W].reshape(bs, ts, HEADS, HD),
                       us[:, COL['a_v']:COL['a_v'] + GROUP_W].reshape(bs, ts, HEADS, HD),
                       us[:, COL['a_ik']:COL['a_ik'] + IDX_DIM].reshape(bs, ts, IDX_DIM),
                       rs, mc, mn, mm, conv_new, gs))
        if last:
            y_s = nxt
        else:
            xn_s = nxt

    sp = [jnp.stack([o[j] for o in outs_p]) for j in range(9)]
    ss = [jnp.stack([o[j] for o in outs_s]) for j in range(9)]
    return (y_p.reshape(bp, tp, D_MODEL), y_s.reshape(bs, ts, D_MODEL), *sp, *ss)
```

```python
import functools
import math

import numpy as np
import jax
import jax.numpy as jnp
from jax import lax
from jax.experimental import pallas as pl
from jax.experimental.pallas import tpu as pltpu

F32 = jnp.float32
I32 = jnp.int32
MXU_DT = jnp.bfloat16
NEG_INF = float("-inf")
HIGHEST = lax.Precision.HIGHEST

D_MODEL = 2048
DEPTH = 4
PAGE = 128
GROUP_W = 512
HEADS = 4
HD = 128
DK = 64
IDX_HEADS = 8
IDX_DIM = 64
TOPK_MAX = 256
CONV_W = 4
LOWRANK = 16
G_TAU = 16.0
ROPE_BASE = 10000.0
CHUNK = 128
PLE_DIM = 256
EPS = 1e-6
LANES = 128
VMEM_LIMIT = 56 * 1024 * 1024

_IN_SPLITS = (
    ('a_q', 512), ('a_k', 512), ('a_v', 512), ('a_iq', 512), ('a_ik', 64), ('a_iw', 8), ('a_z', 512),
    ('r_q', 256), ('r_k', 256), ('r_v', 512), ('r_z', 512),
    ('m_q', 256), ('m_k', 256), ('m_v', 512), ('m_i', 4), ('m_f', 4), ('m_o', 512), ('m_z', 512),
    ('g_q', 256), ('g_k', 256), ('g_v', 512), ('g_lr', 16), ('g_z', 512),
)
_MAIN = ('a_q', 'a_k', 'a_v', 'a_iq', 'a_z', 'r_q', 'r_k', 'r_v', 'r_z',
         'm_q', 'm_k', 'm_v', 'm_o', 'm_z', 'g_q', 'g_k', 'g_v', 'g_z')
_SMALL = ('a_ik', 'a_iw', 'm_i', 'm_f', 'g_lr')


def _layout():
    orig, off = {}, 0
    for name, w in _IN_SPLITS:
        orig[name] = (off, w)
        off += w
    new, off = {}, 0
    for name in _MAIN + _SMALL:
        new[name] = off
        off += orig[name][1]
    return orig, new, off


_ORIG, COL, _USED = _layout()
N_U = 8192
SMALL0 = COL['a_ik']
assert SMALL0 % LANES == 0 and _USED - SMALL0 <= LANES
S_IK, S_IW, S_MI, S_MF, S_LR = (COL[n] - SMALL0 for n in _SMALL)


def _dot(a, b, **kw):
    return jnp.dot(a, b, preferred_element_type=F32, **kw)


def _dot_nt(a, b):
    return lax.dot_general(a, b, (((1,), (1,)), ((), ())), preferred_element_type=F32)


def _dot_tn(a, b):
    return lax.dot_general(a, b, (((0,), (0,)), ((), ())), preferred_element_type=F32)


def _mx(x):
    return x.astype(MXU_DT)


def _split3(x):
    hi = _mx(x)
    r1 = x - hi.astype(F32)
    mid = _mx(r1)
    return hi, mid, _mx(r1 - mid.astype(F32))


def _sum01_left(m01, x):
    m = _mx(m01)
    hi, mid, lo = _split3(x)
    return _dot(m, hi) + _dot(m, mid) + _dot(m, lo)


def _sum01_right(x, m01):
    m = _mx(m01)
    hi, mid, lo = _split3(x)
    return _dot(hi, m) + _dot(mid, m) + _dot(lo, m)


def _cparams(sem):
    return pltpu.CompilerParams(dimension_semantics=sem, vmem_limit_bytes=VMEM_LIMIT)


def _const_spec(shape):
    nd = len(shape)
    return pl.BlockSpec(shape, lambda *_: (0,) * nd)


def _single(shape, index_map):
    return pl.BlockSpec(shape, index_map, pipeline_mode=pl.Buffered(1))


def _rms_kernel(x_ref, g_ref, o_ref):
    x = x_ref[...]
    y = x * lax.rsqrt(jnp.mean(x * x, axis=-1, keepdims=True) + EPS)
    o_ref[...] = (y * g_ref[...]).astype(o_ref.dtype)


def _rmsnorm(x, g, out_dtype):
    m = x.shape[0]
    tm = min(m, 512)
    return pl.pallas_call(
        _rms_kernel, grid=(m // tm,),
        in_specs=[pl.BlockSpec((tm, D_MODEL), lambda i: (i, 0)), _const_spec((1, D_MODEL))],
        out_specs=pl.BlockSpec((tm, D_MODEL), lambda i: (i, 0)),
        out_shape=jax.ShapeDtypeStruct((m, D_MODEL), out_dtype),
        compiler_params=_cparams(("parallel",)), name="rmsnorm",
    )(x, g.reshape(1, D_MODEL))


def _mm_kernel(x_ref, w_ref, o_ref):
    o_ref[...] = _dot(x_ref[...], w_ref[...])


def _inproj(xn, w):
    m = xn.shape[0]
    tm, tn = min(m, 1024), 1024
    return pl.pallas_call(
        _mm_kernel, grid=(N_U // tn, m // tm),
        in_specs=[pl.BlockSpec((tm, D_MODEL), lambda n, i: (i, 0)),
                  pl.BlockSpec((D_MODEL, tn), lambda n, i: (0, n))],
        out_specs=pl.BlockSpec((tm, tn), lambda n, i: (i, n)),
        out_shape=jax.ShapeDtypeStruct((m, N_U), F32),
        compiler_params=_cparams(("parallel", "parallel")), name="inproj",
    )(xn, w)


_INT_MIN = -2147483648


def _key_to_float(cs):
    bits = cs ^ ((cs >> 31) & jnp.int32(0x7FFFFFFF))
    return lax.bitcast_convert_type(bits, F32)


def _select_threshold(count_fn, keff, n_index_bits, full_index):
    zero = jnp.zeros_like(keff)

    def bit_body(bi, ukey):
        trial = ukey | lax.shift_left(jnp.int32(1), 31 - bi)
        cand = _key_to_float(trial ^ jnp.int32(_INT_MIN))
        cnt = count_fn(lambda s, idx: s >= cand)
        return jnp.where(cnt >= keff, trial, ukey)

    ukey = lax.fori_loop(0, 32, bit_body, zero)
    thr = _key_to_float(ukey ^ jnp.int32(_INT_MIN))
    c_gt = count_fn(lambda s, idx: s > thr)
    c_eq = count_fn(lambda s, idx: s == thr)
    need = keff - c_gt
    surplus = jnp.max(c_eq - need) > 0

    def tie_search():
        def body(bi, jm):
            trial = jm | lax.shift_left(jnp.int32(1), n_index_bits - 1 - bi)
            cnt = count_fn(lambda s, idx: (s == thr) & (idx < trial))
            return jnp.where(cnt < need, trial, jm)
        return lax.fori_loop(0, n_index_bits, body, zero)

    jm = lax.cond(surplus, tie_search, lambda: jnp.full_like(keff, full_index))
    return thr, jm


TQ = 128
TK = 512


def _fold8(x):
    parts = [x[r:r + 8] for r in range(0, x.shape[0], 8)]
    while len(parts) > 1:
        parts = [parts[a] + parts[a + 1] for a in range(0, len(parts) - 1, 2)] + (parts[-1:] if len(parts) % 2 else [])
    return parts[0]


def _dsa_prompt_kernel(q_ref, qi_ref, wt_ref, kb_ref, vt_ref, kib_ref, o_ref, sc, *, t_len, topk):
    i = pl.program_id(1)
    nkt = (i * TQ) // TK + 1
    qpos = i * TQ + lax.broadcasted_iota(I32, (1, TQ), 1)
    krow = lax.broadcasted_iota(I32, (TK, TQ), 0)
    wt = wt_ref[S_IW:S_IW + IDX_HEADS, :] * (IDX_DIM ** -0.5)
    qib = _mx(qi_ref[...])

    def score_tile(j, carry):
        kt = kib_ref[pl.ds(pl.multiple_of(j * TK, TK), TK), :]
        acc = jnp.zeros((TK, TQ), F32)
        for h in range(IDX_HEADS):
            s = _dot_nt(kt, qib[:, h * IDX_DIM:(h + 1) * IDX_DIM])
            acc = acc + jnp.maximum(s, 0.0) * wt[h:h + 1, :]
        score = acc * (IDX_HEADS ** -0.5)
        sc[j] = jnp.where(j * TK + krow <= qpos, score, NEG_INF)
        return carry

    lax.fori_loop(0, nkt, score_tile, 0)

    def count_fn(pred):
        def body(j, acc):
            return acc + _fold8(pred(sc[j], j * TK + krow).astype(I32))
        acc = lax.fori_loop(0, nkt, body, jnp.zeros((8, TQ), I32))
        return jnp.sum(acc, axis=0, keepdims=True)

    keff = jnp.minimum(jnp.int32(topk), qpos + 1)
    thr, jm = _select_threshold(count_fn, keff, max(1, (t_len - 1).bit_length()), t_len)

    def bias_tile(j, carry):
        t = sc[j]
        sel = (t > thr) | ((t == thr) & (j * TK + krow <= jm))
        sc[j] = jnp.where(sel, 0.0, NEG_INF)
        return carry

    lax.fori_loop(0, nkt, bias_tile, 0)

    qb = _mx(q_ref[...])

    def attn_tile(j, carry):
        r = pl.ds(pl.multiple_of(j * TK, TK), TK)
        bias = sc[j]
        out = []
        for h in range(HEADS):
            hs = slice(h * HD, (h + 1) * HD)
            m, l, acc = carry[h]
            lg = _dot_nt(kb_ref[r, hs], qb[:, hs]) * (HD ** -0.5) + bias
            m_new = jnp.maximum(m, jnp.max(lg, axis=0, keepdims=True))
            m_safe = jnp.where(m_new == NEG_INF, 0.0, m_new)
            alpha = jnp.exp(m - m_safe)
            p = jnp.exp(lg - m_safe)
            l = alpha * l + jnp.sum(p, axis=0, keepdims=True)
            acc = alpha * acc + _dot(vt_ref[hs, r], _mx(p))
            out.append((m_new, l, acc))
        return tuple(out)

    init = tuple((jnp.full((1, TQ), NEG_INF, F32), jnp.zeros((1, TQ), F32), jnp.zeros((HD, TQ), F32))
                 for _ in range(HEADS))
    res = lax.fori_loop(0, nkt, attn_tile, init)
    for h in range(HEADS):
        _, l, acc = res[h]
        o_ref[:, h * HD:(h + 1) * HD] = (acc / l).T


def _dsa_prompt(u, smt, bsz, t_len):
    nq = t_len // TQ
    topk = min(TOPK_MAX, t_len // 4)
    kern = functools.partial(_dsa_prompt_kernel, t_len=t_len, topk=topk)
    cb = lambda name: COL[name] // GROUP_W
    kb = u[:, COL['a_k']:COL['a_k'] + GROUP_W].astype(MXU_DT)
    vt = u[:, COL['a_v']:COL['a_v'] + GROUP_W].T.astype(MXU_DT)
    kib = u[:, COL['a_ik']:COL['a_ik'] + IDX_DIM].astype(MXU_DT)
    return pl.pallas_call(
        kern, grid=(bsz, nq),
        in_specs=[pl.BlockSpec((TQ, GROUP_W), lambda b, i: (b * nq + i, cb('a_q'))),
                  pl.BlockSpec((TQ, GROUP_W), lambda b, i: (b * nq + i, cb('a_iq'))),
                  pl.BlockSpec((LANES, TQ), lambda b, i: (0, b * nq + i)),
                  _single((t_len, GROUP_W), lambda b, i: (b, 0)),
                  _single((GROUP_W, t_len), lambda b, i: (0, b)),
                  _single((t_len, IDX_DIM), lambda b, i: (b, 0))],
        out_specs=pl.BlockSpec((TQ, GROUP_W), lambda b, i: (b * nq + i, 0)),
        out_shape=jax.ShapeDtypeStruct((bsz * t_len, GROUP_W), F32),
        scratch_shapes=[pltpu.VMEM((t_len // TK, TK, TQ), F32)],
        compiler_params=_cparams(("parallel", "parallel")), name="dsa_prompt",
    )(u, u, smt, kb, vt, kib)


_GLA_LEVELS = (64, 32, 16, 8, 4, 2, 1)


def _gla_constants():
    t = np.arange(CHUNK)
    tri = t[None, :] <= t[:, None]
    mats = [tri]
    masks = [np.eye(CHUNK, dtype=bool)]
    for hs in _GLA_LEVELS:
        bnd = (t // (2 * hs)) * 2 * hs + hs - 1
        mats.append(tri != (t[None, :] <= bnd[:, None]))
        same = (t[:, None] // (2 * hs)) == (t[None, :] // (2 * hs))
        upper_q = ((t // hs) % 2 == 1)[:, None]
        lower_k = ((t // hs) % 2 == 0)[None, :]
        masks.append(same & upper_q & lower_k)
    mats.append(np.ones((CHUNK, CHUNK), bool))
    return (np.concatenate(mats, 0).astype(np.float32), np.stack(masks, 0).astype(np.float32))


def _mixers_kernel(rqk_ref, rv_ref, mqk_ref, mv_ref, gqk_ref, gv_ref, sm_ref, smt_ref,
                   cos_ref, sin_ref, dmat_ref, inter_ref, kdec_ref, gc_ref,
                   convw_ref, convb_ref, gbrow_ref, gbcol_ref, wg2_ref, bg2_ref, cs_ref, pm_ref, triu_ref,
                   ro_ref, mo_ref, go_ref, rs_out, mc_out, mn_out, mm_out, gs_out,
                   rs, mc, mn, mm, gs, cbuf):
    c = pl.program_id(1)

    @pl.when(c == 0)
    def _():
        rs[...] = jnp.zeros(rs.shape, F32)
        mc[...] = jnp.zeros(mc.shape, F32)
        mn[...] = jnp.zeros(mn.shape, F32)
        mm[...] = jnp.zeros(mm.shape, F32)
        gs[...] = jnp.zeros(gs.shape, F32)
        cbuf[0:8, :] = jnp.zeros((8, GROUP_W), F32)

    lane = lax.broadcasted_iota(I32, (CHUNK, LANES), 1)

    cosv, sinv = cos_ref[...], sin_ref[...]

    def rope(x):
        swapped = jnp.where((lane & 32) == 0, pltpu.roll(x, 96, 1), pltpu.roll(x, 32, 1))
        return x * cosv + swapped * sinv

    rqk = rqk_ref[...]
    rq = [rope(rqk[:, 0:128]) * (DK ** -0.5), rope(rqk[:, 128:256]) * (DK ** -0.5)]
    rk = [rope(rqk[:, 256:384]), rope(rqk[:, 384:512])]
    rv = rv_ref[...]
    for h in range(HEADS):
        ls = slice((h % 2) * DK, (h % 2 + 1) * DK)
        hs = slice(h * HD, (h + 1) * HD)
        qh, kh, vh = _mx(rq[h // 2][:, ls]), rk[h // 2][:, ls], _mx(rv[:, hs])
        att = _dot_nt(qh, _mx(kh)) * dmat_ref[h]
        s_old = rs[h]
        ro_ref[:, hs] = _dot(_mx(att), vh) + _dot(qh, _mx(s_old)) * inter_ref[h]
        rs[h] = s_old * gc_ref[h, 0:1, :] + _dot_tn(_mx(kh * kdec_ref[h]), vh)

    cbuf[8:8 + CHUNK, :] = mqk_ref[...]
    y = convb_ref[...]
    for j in range(CONV_W):
        y = y + cbuf[pl.ds(8 - (CONV_W - 1) + j, CHUNK), :] * convw_ref[j:j + 1, :]
    cbuf[0:8, :] = cbuf[CHUNK:CHUNK + 8, :]
    y = y * jax.nn.sigmoid(y)
    smp = sm_ref[...] + gbrow_ref[...]
    smtp = smt_ref[...] + gbcol_ref[...]
    bcol_all = _sum01_left(cs_ref[0:CHUNK, :], jax.nn.log_sigmoid(smp))
    brow_all = _sum01_right(jax.nn.log_sigmoid(smtp), triu_ref[...])
    tri = cs_ref[0:CHUNK, :] > 0.5
    mv = mv_ref[...]
    for h in range(HEADS):
        hs = slice(h * HD, (h + 1) * HD)
        qh = _mx(y[:, h * DK:(h + 1) * DK])
        kh = y[:, 256 + h * DK:256 + (h + 1) * DK] * (DK ** -0.5)
        vh = mv[:, hs]
        b_col = bcol_all[:, S_MF + h:S_MF + h + 1]
        i_col = smp[:, S_MI + h:S_MI + h + 1]
        b_row = brow_all[S_MF + h:S_MF + h + 1, :]
        i_row = smtp[S_MI + h:S_MI + h + 1, :]
        m_prev = mm[h][0:1, 0:1]
        dlog = jnp.where(tri, b_col - b_row + i_row, NEG_INF)
        m_inter = b_col + m_prev
        m_t = jnp.maximum(m_inter, jnp.max(dlog, axis=1, keepdims=True))
        wts = jnp.exp(dlog - m_t)
        scale = jnp.exp(m_inter - m_t)
        qk = _dot_nt(qh, _mx(kh)) * wts
        c_old, n_old = mc[h], mn[h][0:1, :]
        num = _dot(_mx(qk), _mx(vh)) + scale * _dot_nt(qh, _mx(c_old))
        qf = y[:, h * DK:(h + 1) * DK]
        dsum = jnp.sum(qk, axis=1, keepdims=True) + scale * jnp.sum(qf * n_old, axis=1, keepdims=True)
        den = jnp.maximum(jnp.abs(dsum), jnp.exp(-m_t))
        mo_ref[:, hs] = num / den
        m_new = m_t[CHUNK - 1:CHUNK, :]
        s_last = scale[CHUNK - 1:CHUNK, :]
        b_last = b_col[CHUNK - 1:CHUNK, :]
        w_last = jnp.exp(b_last - b_col + i_col - m_new)
        mc[h] = s_last * c_old + _dot_tn(_mx(vh * w_last), _mx(kh))
        mn[h] = jnp.broadcast_to(s_last * n_old + jnp.sum(kh * w_last, axis=0, keepdims=True), (8, DK))
        mm[h] = jnp.broadcast_to(m_new, (8, LANES))

    glr = sm_ref[:, S_LR:S_LR + LOWRANK]
    la = jax.nn.log_sigmoid(_dot(_mx(glr), _mx(wg2_ref[...])) + bg2_ref[...]) / G_TAU
    ball = _sum01_left(cs_ref[...], la)
    nlev = len(_GLA_LEVELS)
    b = ball[0:CHUNK]
    b_tot = ball[(nlev + 1) * CHUNK:(nlev + 2) * CHUNK]
    gqk = gqk_ref[...]
    gq = gqk[:, 0:256] * (DK ** -0.5)
    gk = gqk[:, 256:512]
    qf_l, kf_l = [gq], [gk]
    for lv in range(nlev):
        dec = jnp.exp(ball[(lv + 1) * CHUNK:(lv + 2) * CHUNK])
        qf_l.append(gq * dec)
        kf_l.append(gk * dec)
    qe = gq * jnp.exp(b)
    kl = gk * jnp.exp(b_tot - b)
    eb = jnp.exp(b_tot[0:1, :])
    gv = gv_ref[...]
    for h in range(HEADS):
        ds_ = slice(h * DK, (h + 1) * DK)
        hs = slice(h * HD, (h + 1) * HD)
        att = jnp.zeros((CHUNK, CHUNK), F32)
        for lv in range(nlev + 1):
            att = att + pm_ref[lv] * _dot_nt(_mx(qf_l[lv][:, ds_]), _mx(kf_l[lv][:, ds_]))
        vh = _mx(gv[:, hs])
        st_old = gs[h]
        go_ref[:, hs] = _dot(_mx(att), vh) + _dot_nt(_mx(qe[:, ds_]), _mx(st_old))
        gs[h] = st_old * eb[:, ds_] + _dot_tn(vh, _mx(kl[:, ds_]))

    @pl.when(c == pl.num_programs(1) - 1)
    def _():
        rs_out[...] = rs[...]
        mc_out[...] = mc[...]
        mn_out[...] = mn[...]
        mm_out[...] = mm[...]
        gs_out[...] = gs[...]


def _retention_tables(chunk):
    log_g = jnp.log1p(-jnp.exp2(-5.0 - jnp.arange(HEADS, dtype=F32)))
    idx = jnp.arange(chunk, dtype=F32)
    diff = idx[:, None] - idx[None, :]
    causal = diff >= 0
    dmat = jnp.where(causal[None], jnp.exp(jnp.where(causal, diff, 0.0)[None] * log_g[:, None, None]), 0.0)
    inter = jnp.exp((idx + 1.0)[None, :] * log_g[:, None])
    kdec = jnp.exp((chunk - 1.0 - idx)[None, :] * log_g[:, None])
    g_c = jnp.exp(chunk * log_g)
    return dmat, inter, kdec, g_c


def _rope_tables(pos):
    half = DK // 2
    inv = jnp.exp(-math.log(ROPE_BASE) * jnp.arange(half, dtype=F32) * (2.0 / DK))
    ang = pos.astype(F32)[:, None] * inv[None, :]
    return jnp.cos(ang), jnp.sin(ang)


def _mixers_prompt(u, smt, bsz, t_len, lw):
    nc = t_len // CHUNK
    cos, sin = _rope_tables(jnp.arange(t_len))
    cosf = jnp.tile(jnp.concatenate([cos, cos], -1), (1, 2))
    sinf = jnp.tile(jnp.concatenate([-sin, sin], -1), (1, 2))
    dmat, inter, kdec, g_c = _retention_tables(CHUNK)
    inter_b = jnp.broadcast_to(inter[:, :, None], (HEADS, CHUNK, HD))
    kdec_b = jnp.broadcast_to(kdec[:, :, None], (HEADS, CHUNK, DK))
    gc_b = jnp.broadcast_to(g_c[:, None, None], (HEADS, 8, HD))
    cs_np, pm_np = _gla_constants()
    cs, pm = jnp.asarray(cs_np), jnp.asarray(pm_np)
    triu = jnp.asarray(cs_np[0:CHUNK].T.copy())
    gb = jnp.zeros((LANES,), F32).at[S_MI:S_MI + HEADS].set(lw['i_b']).at[S_MF:S_MF + HEADS].set(lw['f_b'])
    rowblk = lambda name: (lambda b, c, n=COL[name] // GROUP_W: (b * nc + c, n))
    cspec = lambda shape: _const_spec(shape)
    m = bsz * t_len
    st = lambda shape: pl.BlockSpec((None,) + shape, lambda b, c: (b,) + (0,) * len(shape))
    outs = pl.pallas_call(
        _mixers_kernel, grid=(bsz, nc),
        in_specs=[pl.BlockSpec((CHUNK, GROUP_W), rowblk('r_q')), pl.BlockSpec((CHUNK, GROUP_W), rowblk('r_v')),
                  pl.BlockSpec((CHUNK, GROUP_W), rowblk('m_q')), pl.BlockSpec((CHUNK, GROUP_W), rowblk('m_v')),
                  pl.BlockSpec((CHUNK, GROUP_W), rowblk('g_q')), pl.BlockSpec((CHUNK, GROUP_W), rowblk('g_v')),
                  pl.BlockSpec((CHUNK, LANES), lambda b, c: (b * nc + c, SMALL0 // LANES)),
                  pl.BlockSpec((LANES, CHUNK), lambda b, c: (0, b * nc + c)),
                  pl.BlockSpec((CHUNK, LANES), lambda b, c: (c, 0)), pl.BlockSpec((CHUNK, LANES), lambda b, c: (c, 0)),
                  cspec((HEADS, CHUNK, CHUNK)), cspec((HEADS, CHUNK, HD)), cspec((HEADS, CHUNK, DK)),
                  cspec((HEADS, 8, HD)),
                  cspec((CONV_W, GROUP_W)), cspec((1, GROUP_W)), cspec((1, LANES)), cspec((LANES, 1)),
                  cspec((LOWRANK, 256)), cspec((1, 256)), cspec(cs.shape), cspec(pm.shape), cspec((CHUNK, CHUNK))],
        out_specs=[pl.BlockSpec((CHUNK, GROUP_W), lambda b, c: (b * nc + c, 0))] * 3
        + [st((HEADS, DK, HD)), st((HEADS, HD, DK)), st((HEADS, 8, DK)), st((HEADS, 8, LANES)), st((HEADS, HD, DK))],
        out_shape=[jax.ShapeDtypeStruct((m, GROUP_W), F32)] * 3
        + [jax.ShapeDtypeStruct((bsz, HEADS, DK, HD), F32), jax.ShapeDtypeStruct((bsz, HEADS, HD, DK), F32),
           jax.ShapeDtypeStruct((bsz, HEADS, 8, DK), F32), jax.ShapeDtypeStruct((bsz, HEADS, 8, LANES), F32),
           jax.ShapeDtypeStruct((bsz, HEADS, HD, DK), F32)],
        scratch_shapes=[pltpu.VMEM((HEADS, DK, HD), F32), pltpu.VMEM((HEADS, HD, DK), F32),
                        pltpu.VMEM((HEADS, 8, DK), F32), pltpu.VMEM((HEADS, 8, LANES), F32),
                        pltpu.VMEM((HEADS, HD, DK), F32), pltpu.VMEM((CHUNK + 8, GROUP_W), F32)],
        compiler_params=_cparams(("arbitrary", "arbitrary")), name="mixers_prompt",
    )(u, u, u, u, u, u, u, smt, cosf, sinf, dmat, inter_b, kdec_b, gc_b,
      lw['conv_w'], lw['conv_b'].reshape(1, GROUP_W), gb.reshape(1, LANES), gb.reshape(LANES, 1),
      lw['wg2'], lw['bg2'].reshape(1, 256), cs, pm, triu)
    r_o, m_o, g_o, rs, mc, mn, mm, gst = outs
    return r_o, m_o, g_o, rs, mc, mn[:, :, 0, :], mm[:, :, 0, 0], jnp.swapaxes(gst, -1, -2)


def _head_norm(x, center):
    outs = []
    for h in range(HEADS):
        xh = x[:, h * HD:(h + 1) * HD]
        if center:
            xh = xh - jnp.mean(xh, axis=-1, keepdims=True)
        outs.append(xh * lax.rsqrt(jnp.mean(xh * xh, axis=-1, keepdims=True) + EPS))
    return jnp.concatenate(outs, axis=-1)


def _silu(x):
    return x * jax.nn.sigmoid(x)


def _out_kernel(ao_ref, ro_ref, mo_ref, go_ref, az_ref, rz_ref, mg_ref, mz_ref, gz_ref, h_ref, p_ref,
                rg_ref, mgn_ref, gg_ref, wout_ref, plew_ref, plegw_ref, gn_ref, hn_ref, xn_ref):
    a_mix = ao_ref[...] * _silu(az_ref[...])
    r_mix = _head_norm(ro_ref[...], True) * rg_ref[...] * _silu(rz_ref[...])
    m_h = mo_ref[...] * jax.nn.sigmoid(mg_ref[...])
    m_mix = _head_norm(m_h, True) * mgn_ref[...] * _silu(mz_ref[...])
    g_mix = _head_norm(go_ref[...], False) * gg_ref[...] * _silu(gz_ref[...])
    h1 = h_ref[...]
    for g, mix in enumerate((a_mix, r_mix, m_mix, g_mix)):
        h1 = h1 + _dot(_mx(mix), wout_ref[g * GROUP_W:(g + 1) * GROUP_W, :])
    gate = jax.nn.sigmoid(_dot(_mx(h1), plegw_ref[...]))
    h2 = h1 + gate * _dot(_mx(p_ref[...]), plew_ref[...])
    hn_ref[...] = h2
    y = h2 * lax.rsqrt(jnp.mean(h2 * h2, axis=-1, keepdims=True) + EPS)
    xn_ref[...] = (y * gn_ref[...]).astype(xn_ref.dtype)


def _out_stage(a_o, r_o, m_o, g_o, u, h, p, lw, g_next, next_dtype):
    m = h.shape[0]
    tm = min(m, 256)
    rowb = lambda i: (i, 0)
    ub = lambda name: (lambda i, n=COL[name] // GROUP_W: (i, n))
    tile = lambda w: pl.BlockSpec((tm, w), rowb)
    return pl.pallas_call(
        _out_kernel, grid=(m // tm,),
        in_specs=[tile(GROUP_W)] * 4
        + [pl.BlockSpec((tm, GROUP_W), ub(n)) for n in ('a_z', 'r_z', 'm_o', 'm_z', 'g_z')]
        + [tile(D_MODEL), tile(PLE_DIM), _const_spec((1, GROUP_W)), _const_spec((1, GROUP_W)), _const_spec((1, GROUP_W)),
           _single((D_MODEL, D_MODEL), lambda i: (0, 0)), _single((PLE_DIM, D_MODEL), lambda i: (0, 0)),
           _single((D_MODEL, D_MODEL), lambda i: (0, 0)), _const_spec((1, D_MODEL))],
        out_specs=[tile(D_MODEL), tile(D_MODEL)],
        out_shape=[jax.ShapeDtypeStruct((m, D_MODEL), F32), jax.ShapeDtypeStruct((m, D_MODEL), next_dtype)],
        compiler_params=_cparams(("parallel",)), name="out_stage",
    )(a_o, r_o, m_o, g_o, u, u, u, u, u, h, p,
      lw['rg'].reshape(1, GROUP_W), lw['mg'].reshape(1, GROUP_W), lw['gg'].reshape(1, GROUP_W),
      lw['w_out'], lw['ple_w'], lw['ple_gw'], g_next.reshape(1, D_MODEL))


def _sample_scores_kernel(pt_ref, qi_ref, ust_ref, *refs, bsz):
    pages, o_ref = refs[:bsz], refs[bsz]
    j = pl.program_id(0)
    for b in range(bsz):
        qrow = qi_ref[b:b + 1, :]
        qh = _mx(jnp.concatenate([qrow[:, h * IDX_DIM:(h + 1) * IDX_DIM] for h in range(IDX_HEADS)], axis=0))
        s = _dot_nt(qh, _mx(pages[b][...])) * (IDX_DIM ** -0.5)
        w_col = ust_ref[S_IW:S_IW + IDX_HEADS, b:b + 1]
        score = jnp.sum(jnp.maximum(s, 0.0) * w_col, axis=0, keepdims=True) * (IDX_HEADS ** -0.5)
        o_ref[b, pl.ds(j, 1), :] = score


def _sample_scores(cache_kidx, page_table, u_s, ust_small, layer):
    bsz, n_pages = page_table.shape
    kern = functools.partial(_sample_scores_kernel, bsz=bsz)
    page_spec = lambda b: pl.BlockSpec((None, None, PAGE, IDX_DIM), lambda j, pt, b=b: (layer, pt[b, j], 0, 0))
    return pl.pallas_call(
        kern,
        grid_spec=pltpu.PrefetchScalarGridSpec(
            num_scalar_prefetch=1, grid=(n_pages,),
            in_specs=[pl.BlockSpec((bsz, GROUP_W), lambda j, pt: (0, COL['a_iq'] // GROUP_W)),
                      pl.BlockSpec((LANES, bsz), lambda j, pt: (0, 0))]
            + [page_spec(b) for b in range(bsz)],
            out_specs=pl.BlockSpec((bsz, n_pages, PAGE), lambda j, pt: (0, 0, 0))),
        out_shape=jax.ShapeDtypeStruct((bsz, n_pages, PAGE), F32),
        compiler_params=_cparams(("arbitrary",)), name="sample_scores",
    )(page_table, u_s, ust_small, *([cache_kidx] * bsz))


def _sample_select_kernel(sc_ref, qi_ref, sm_ref, pg_ref, off_ref, sbias_ref, nbias_ref, *, n_pages, topk):
    sc = sc_ref[...]
    bsz = sc.shape[0]
    past = n_pages * PAGE
    qi, sm = qi_ref[...], sm_ref[...]
    ki = sm[:, :, S_IK:S_IK + IDX_DIM]
    acc = jnp.zeros((bsz, 1, 1), F32)
    for h in range(IDX_HEADS):
        s = jnp.sum(qi[:, :, h * IDX_DIM:(h + 1) * IDX_DIM] * ki, axis=2, keepdims=True) * (IDX_DIM ** -0.5)
        acc = acc + jnp.maximum(s, 0.0) * sm[:, :, S_IW + h:S_IW + h + 1]
    s_new = acc * (IDX_HEADS ** -0.5)
    idx = (lax.broadcasted_iota(I32, sc.shape, 1) * PAGE + lax.broadcasted_iota(I32, sc.shape, 2))

    def count_fn(pred):
        c = jnp.sum(pred(sc, idx).astype(I32), axis=2, keepdims=True)
        c = jnp.sum(c, axis=1, keepdims=True)
        return c + pred(s_new, jnp.int32(past)).astype(I32)

    keff = jnp.full((bsz, 1, 1), min(topk, past + 1), I32)
    thr, jm = _select_threshold(count_fn, keff, max(1, past.bit_length()), past + 1)
    sel = (sc > thr) | ((sc == thr) & (idx <= jm))
    sel_new = (s_new > thr) | ((s_new == thr) & (jnp.int32(past) <= jm))
    nbias_ref[...] = jnp.broadcast_to(jnp.where(sel_new, 0.0, NEG_INF), nbias_ref.shape)

    iot = lambda shape, d: lax.broadcasted_iota(I32, shape, d)
    u_off = (iot((PAGE, PAGE), 0) <= iot((PAGE, PAGE), 1)).astype(F32)
    u_pg = (iot((n_pages, n_pages), 0) <= iot((n_pages, n_pages), 1)).astype(F32)
    r_col = iot((topk, 1), 0).astype(F32)
    lane_pg = iot((topk, n_pages), 1).astype(F32)
    ones8 = jnp.ones((8, PAGE), F32)
    for b in range(bsz):
        m = sel[b].astype(F32)
        c = _dot(_mx(m), _mx(u_off))
        tot = _dot_nt(_mx(ones8), _mx(m))
        pin = _dot(_mx(tot), _mx(u_pg))
        pin1, pex1 = pin[0:1, :], (pin - tot)[0:1, :]
        n_sel = jnp.max(pin1, axis=1, keepdims=True)
        page_of = jnp.sum((pin1 <= r_col).astype(F32), axis=1, keepdims=True)
        onehot = (lane_pg == page_of).astype(F32)
        crow = _dot(_mx(onehot), _mx(c))
        rank_in_page = r_col - jnp.sum(onehot * pex1, axis=1, keepdims=True)
        off_of = jnp.sum((crow <= rank_in_page).astype(F32), axis=1, keepdims=True)
        valid = r_col < n_sel
        pg_ref[b] = jnp.where(valid, page_of, 0.0).astype(I32)
        off_ref[b] = jnp.where(valid, off_of, 0.0).astype(I32)
        sbias_ref[b] = jnp.where(valid, 0.0, NEG_INF)


def _sample_select(scores, u_s, topk):
    bsz, n_pages, _ = scores.shape
    kern = functools.partial(_sample_select_kernel, n_pages=n_pages, topk=topk)
    col = (bsz, topk, 1)
    pg, off, sbias, nbias = pl.pallas_call(
        kern, grid=(1,),
        in_specs=[_const_spec(scores.shape),
                  pl.BlockSpec((bsz, 1, GROUP_W), lambda i: (0, 0, COL['a_iq'] // GROUP_W)),
                  pl.BlockSpec((bsz, 1, LANES), lambda i: (0, 0, SMALL0 // LANES))],
        out_specs=[_const_spec(col), _const_spec(col), _const_spec(col), _const_spec((bsz, 1, LANES))],
        out_shape=[jax.ShapeDtypeStruct(col, I32), jax.ShapeDtypeStruct(col, I32), jax.ShapeDtypeStruct(col, F32),
                   jax.ShapeDtypeStruct((bsz, 1, LANES), F32)],
        compiler_params=_cparams(("arbitrary",)), name="sample_select",
    )(scores, u_s.reshape(bsz, 1, N_U), u_s.reshape(bsz, 1, N_U))
    return pg.reshape(bsz, topk), off.reshape(bsz, topk), sbias, nbias


def _sample_attn_kernel(pt_ref, pg_ref, off_ref, q_ref, knew_ref, vnew_ref, sbias_ref, nbias_ref, ck_ref, cv_ref,
                        o_ref, kbuf, vbuf, sem, *, layer, topk):
    b = pl.program_id(0)

    def row_copies(r):
        page = pt_ref[b, pg_ref[b, r]]
        off = off_ref[b, r]
        return (pltpu.make_async_copy(ck_ref.at[layer, page, off], kbuf.at[r], sem.at[0]),
                pltpu.make_async_copy(cv_ref.at[layer, page, off], vbuf.at[r], sem.at[1]))

    def start(r, carry):
        for cp in row_copies(r):
            cp.start()
        return carry

    def wait(r, carry):
        for cp in row_copies(r):
            cp.wait()
        return carry

    lax.fori_loop(0, topk, start, 0)
    lax.fori_loop(0, topk, wait, 0)

    bsz = knew_ref.shape[0]
    q = q_ref[0]
    own = lax.broadcasted_iota(I32, (bsz, 1), 0) == b
    nb = jnp.where(own, nbias_ref[0][:, 0:1], NEG_INF)
    sb = sbias_ref[0]
    for h in range(HEADS):
        hs = slice(h * HD, (h + 1) * HD)
        qh = q[:, hs]
        lg = jnp.sum(kbuf[:, h, :] * qh, axis=1, keepdims=True) * (HD ** -0.5) + sb
        lg_n = jnp.sum(knew_ref[:, hs] * qh, axis=1, keepdims=True) * (HD ** -0.5) + nb
        m = jnp.maximum(jnp.max(lg, axis=0, keepdims=True), jnp.max(lg_n, axis=0, keepdims=True))
        p, p_n = jnp.exp(lg - m), jnp.exp(lg_n - m)
        l = jnp.sum(p, axis=0, keepdims=True) + jnp.sum(p_n, axis=0, keepdims=True)
        o = (jnp.sum(p * vbuf[:, h, :], axis=0, keepdims=True)
             + jnp.sum(p_n * vnew_ref[:, hs], axis=0, keepdims=True))
        o_ref[0, :, hs] = o / l


def _sample_attn(cache_k, cache_v, page_table, u_s, pg, off, sbias, nbias, layer):
    bsz = page_table.shape[0]
    topk = pg.shape[1]
    kern = functools.partial(_sample_attn_kernel, layer=layer, topk=topk)
    ucol = lambda name: pl.BlockSpec((bsz, GROUP_W), lambda b, *_, n=COL[name] // GROUP_W: (0, n))
    return pl.pallas_call(
        kern,
        grid_spec=pltpu.PrefetchScalarGridSpec(
            num_scalar_prefetch=3, grid=(bsz,),
            in_specs=[pl.BlockSpec((1, 1, GROUP_W), lambda b, *_: (b, 0, COL['a_q'] // GROUP_W)),
                      ucol('a_k'), ucol('a_v'),
                      pl.BlockSpec((1, topk, 1), lambda b, *_: (b, 0, 0)),
                      pl.BlockSpec((1, 1, LANES), lambda b, *_: (b, 0, 0)),
                      pl.BlockSpec(memory_space=pl.ANY), pl.BlockSpec(memory_space=pl.ANY)],
            out_specs=pl.BlockSpec((1, 1, GROUP_W), lambda b, *_: (b, 0, 0)),
            scratch_shapes=[pltpu.VMEM((topk, HEADS, HD), F32), pltpu.VMEM((topk, HEADS, HD), F32),
                            pltpu.SemaphoreType.DMA((2,))]),
        out_shape=jax.ShapeDtypeStruct((bsz, 1, GROUP_W), F32),
        compiler_params=_cparams(("arbitrary",)), name="sample_attn",
    )(page_table, pg, off, u_s.reshape(bsz, 1, N_U), u_s, u_s, sbias, nbias, cache_k, cache_v).reshape(bsz, GROUP_W)


def _decode_kernel(us_ref, ust_ref, rs_ref, mc_ref, mct_ref, mn_ref, mm_ref, cb_ref, cbt_ref, gs_ref,
                   cosr_ref, sinr_ref, cosc_ref, sinc_ref, gam_ref,
                   convw_ref, convb_ref, convwt_ref, convbt_ref, ib_ref, fb_ref, wg2_ref, bg2_ref, wg2t_ref, bg2t_ref,
                   ro_ref, mo_ref, go_ref, rs_out, mc_out, mn_out, mm_out, gs_out, *, bsz):
    us = lambda name, w: us_ref[:, COL[name]:COL[name] + w]
    ut = lambda name, w: ust_ref[COL[name]:COL[name] + w, :]
    lane = lax.broadcasted_iota(I32, (bsz, LANES), 1)

    def rope_row(x):
        swapped = jnp.where((lane & 32) == 0, pltpu.roll(x, 96, 1), pltpu.roll(x, 32, 1))
        return x * cosr_ref[...] + swapped * sinr_ref[...]

    def rope_col(x):
        x1, x2 = x[0:32, :], x[32:64, :]
        cc, ss = cosc_ref[...], sinc_ref[...]
        return jnp.concatenate([x1 * cc - x2 * ss, x1 * ss + x2 * cc], axis=0)

    rq, rk = us('r_q', 256), us('r_k', 256)
    rq_row = jnp.concatenate([rope_row(rq[:, 0:128]), rope_row(rq[:, 128:256])], axis=1) * (DK ** -0.5)
    rk_row = jnp.concatenate([rope_row(rk[:, 0:128]), rope_row(rk[:, 128:256])], axis=1)
    rv = us('r_v', 512)
    rqt, rkt = ut('r_q', 256), ut('r_k', 256)
    for h in range(HEADS):
        ds_ = slice(h * DK, (h + 1) * DK)
        hs = slice(h * HD, (h + 1) * HD)
        gam = gam_ref[h, 0:1, :]
        qk = jnp.sum(rq_row[:, ds_] * rk_row[:, ds_], axis=1, keepdims=True)
        intra = qk * rv[:, hs]
        q_col = rope_col(rqt[ds_, :]) * (DK ** -0.5)
        k_col = rope_col(rkt[ds_, :])
        rows = []
        for b in range(bsz):
            s_old = rs_ref[b, h]
            rows.append(jnp.sum(q_col[:, b:b + 1] * s_old, axis=0, keepdims=True))
            rs_out[b, h] = s_old * gam + k_col[:, b:b + 1] * rv[b:b + 1, hs]
        ro_ref[:, hs] = intra + jnp.concatenate(rows, axis=0) * gam

    y = convb_ref[...]
    yt = convbt_ref[...]
    for j in range(CONV_W - 1):
        y = y + cb_ref[j] * convw_ref[j:j + 1, :]
        yt = yt + cbt_ref[j] * convwt_ref[:, j:j + 1]
    y = y + us('m_q', 512) * convw_ref[CONV_W - 1:CONV_W, :]
    yt = yt + ut('m_q', 512) * convwt_ref[:, CONV_W - 1:CONV_W]
    y = y * jax.nn.sigmoid(y)
    yt = yt * jax.nn.sigmoid(yt)
    mv = us('m_v', 512)
    mvt = ut('m_v', 512)
    i_all = us('m_i', HEADS) + ib_ref[...]
    lf_all = jax.nn.log_sigmoid(us('m_f', HEADS) + fb_ref[...])
    for h in range(HEADS):
        ds_ = slice(h * DK, (h + 1) * DK)
        hs = slice(h * HD, (h + 1) * HD)
        q_row = y[:, ds_]
        k_row = y[:, 256 + h * DK:256 + (h + 1) * DK] * (DK ** -0.5)
        q_col = yt[ds_, :]
        i_h, lf_h, m0 = i_all[:, h:h + 1], lf_all[:, h:h + 1], mm_ref[:, h:h + 1]
        m_inter = lf_h + m0
        m_t = jnp.maximum(m_inter, i_h)
        wts = jnp.exp(i_h - m_t)
        scale = jnp.exp(m_inter - m_t)
        qk = jnp.sum(q_row * k_row, axis=1, keepdims=True) * wts
        n_old = mn_ref[:, ds_]
        rows = []
        scale_b = jnp.broadcast_to(scale, (bsz, DK))
        kw = k_row * wts
        for b in range(bsz):
            rows.append(jnp.sum(q_col[:, b:b + 1] * mct_ref[b, h], axis=0, keepdims=True))
            mc_out[b, h] = mc_ref[b, h] * scale_b[b:b + 1, :] + mvt[hs, b:b + 1] * kw[b:b + 1, :]
        num = qk * mv[:, hs] + scale * jnp.concatenate(rows, axis=0)
        dsum = qk + scale * jnp.sum(n_old * q_row, axis=1, keepdims=True)
        den = jnp.maximum(jnp.abs(dsum), jnp.exp(-m_t))
        mo_ref[:, hs] = num / den
        mn_out[:, ds_] = scale * n_old + kw
        mm_out[:, h:h + 1] = m_t

    la_col = jax.nn.log_sigmoid(_dot(_mx(wg2t_ref[...]), _mx(ut('g_lr', LOWRANK))) + bg2t_ref[...]) / G_TAU
    gq_row, gk_row = us('g_q', 256) * (DK ** -0.5), us('g_k', 256)
    gqt, gkt = ut('g_q', 256) * (DK ** -0.5), ut('g_k', 256)
    gv = us('g_v', 512)
    ea_col = jnp.exp(la_col)
    for h in range(HEADS):
        ds_ = slice(h * DK, (h + 1) * DK)
        hs = slice(h * HD, (h + 1) * HD)
        qk = jnp.sum(gq_row[:, ds_] * gk_row[:, ds_], axis=1, keepdims=True)
        qe_col = gqt[ds_, :] * ea_col[ds_, :]
        rows = []
        for b in range(bsz):
            s_old = gs_ref[b, h]
            rows.append(jnp.sum(qe_col[:, b:b + 1] * s_old, axis=0, keepdims=True))
            gs_out[b, h] = ea_col[ds_, b:b + 1] * s_old + gkt[ds_, b:b + 1] * gv[b:b + 1, hs]
        go_ref[:, hs] = qk * gv[:, hs] + jnp.concatenate(rows, axis=0)


def _decode_step(u_s, ust, st, lw, past):
    bsz = u_s.shape[0]
    ret_s, m_c, m_n, m_m, conv_buf, gla_s = st
    cos, sin = _rope_tables(past + jnp.arange(1))
    cosr = jnp.tile(jnp.concatenate([cos, cos], -1), (1, 2))
    sinr = jnp.tile(jnp.concatenate([-sin, sin], -1), (1, 2))
    _, _, _, g_c = _retention_tables(1)
    gam = jnp.broadcast_to(g_c[:, None, None], (HEADS, 8, HD))
    full = lambda a: _const_spec(a.shape)
    args = [u_s, ust, ret_s, m_c, jnp.swapaxes(m_c, -1, -2), m_n.reshape(bsz, HEADS * DK), m_m,
            jnp.swapaxes(conv_buf, 0, 1), jnp.transpose(conv_buf, (1, 2, 0)), gla_s,
            cosr, sinr, cos.reshape(DK // 2, 1), sin.reshape(DK // 2, 1), gam,
            lw['conv_w'], lw['conv_b'].reshape(1, GROUP_W), lw['conv_w'].T, lw['conv_b'].reshape(GROUP_W, 1),
            lw['i_b'].reshape(1, HEADS), lw['f_b'].reshape(1, HEADS),
            lw['wg2'], lw['bg2'].reshape(1, 256), lw['wg2'].T, lw['bg2'].reshape(256, 1)]
    outs = pl.pallas_call(
        functools.partial(_decode_kernel, bsz=bsz), grid=(1,),
        in_specs=[full(a) for a in args],
        out_specs=[_const_spec((bsz, GROUP_W))] * 3
        + [full(ret_s), full(m_c), _const_spec((bsz, HEADS * DK)), full(m_m), full(gla_s)],
        out_shape=[jax.ShapeDtypeStruct((bsz, GROUP_W), F32)] * 3
        + [jax.ShapeDtypeStruct(ret_s.shape, F32), jax.ShapeDtypeStruct(m_c.shape, F32),
           jax.ShapeDtypeStruct((bsz, HEADS * DK), F32), jax.ShapeDtypeStruct(m_m.shape, F32),
           jax.ShapeDtypeStruct(gla_s.shape, F32)],
        compiler_params=_cparams(("arbitrary",)), name="decode_step",
    )(*args)
    r_o, m_o, g_o, rs, mc, mn, mm, gs = outs
    return r_o, m_o, g_o, rs, mc, mn.reshape(bsz, HEADS, DK), mm, gs


def _prep_w_in(w):
    cols = [w[:, _ORIG[n][0]:_ORIG[n][0] + _ORIG[n][1]] for n in _MAIN + _SMALL]
    cols.append(jnp.zeros((w.shape[0], N_U - _USED), w.dtype))
    return jnp.concatenate(cols, axis=1).astype(MXU_DT)


def kernel(x_prompt, x_sample, cache_k, cache_v, cache_kidx, page_table, state_ret, state_mlstm_c, state_mlstm_n,
           state_mlstm_m, state_mlstm_conv, state_gla, p_prompt, p_sample, norm_g, w_in, conv_w, conv_b,
           m_igate_b, m_fgate_b, gla_wg2, gla_bg2, ret_norm_g, mlstm_norm_g, gla_norm_g, w_out, ple_w,
           ple_gate_w, final_g):
    bp, tp, _ = x_prompt.shape
    bs, ts, _ = x_sample.shape
    assert ts == 1 and tp % TK == 0 and tp % CHUNK == 0
    n_pages = page_table.shape[1]
    past = n_pages * PAGE
    topk_s = min(TOPK_MAX, (past + ts) // 4)
    mp, ms = bp * tp, bs * ts

    hp = x_prompt.reshape(mp, D_MODEL)
    hs = x_sample.reshape(ms, D_MODEL)
    xn_p = _rmsnorm(hp, norm_g[0], MXU_DT)
    xn_s = _rmsnorm(hs, norm_g[0], MXU_DT)
    outs_p, outs_s = [], []
    y_p = y_s = None
    for l in range(DEPTH):
        lw = dict(conv_w=conv_w[l], conv_b=conv_b[l], i_b=m_igate_b[l], f_b=m_fgate_b[l], wg2=gla_wg2[l],
                  bg2=gla_bg2[l], rg=ret_norm_g[l], mg=mlstm_norm_g[l], gg=gla_norm_g[l],
                  w_out=w_out[l].astype(MXU_DT), ple_w=ple_w[l].astype(MXU_DT), ple_gw=ple_gate_w[l].astype(MXU_DT))
        w_l = _prep_w_in(w_in[l])
        last = l == DEPTH - 1
        g_next = final_g if last else norm_g[l + 1]
        nd = F32 if last else MXU_DT

        u = _inproj(xn_p, w_l)
        smt = u[:, SMALL0:SMALL0 + LANES].T
        a_o = _dsa_prompt(u, smt, bp, tp)
        r_o, m_o, g_o, rs, mc, mn, mm, gs = _mixers_prompt(u, smt, bp, tp, lw)
        hp, nxt = _out_stage(a_o, r_o, m_o, g_o, u, hp, p_prompt[l].reshape(mp, PLE_DIM), lw, g_next, nd)
        u3 = u.reshape(bp, tp, N_U)
        outs_p.append((u3[:, :, COL['a_k']:COL['a_k'] + GROUP_W].reshape(bp, tp, HEADS, HD),
                       u3[:, :, COL['a_v']:COL['a_v'] + GROUP_W].reshape(bp, tp, HEADS, HD),
                       u3[:, :, COL['a_ik']:COL['a_ik'] + IDX_DIM],
                       rs, mc, mn, mm, u3[:, tp - (CONV_W - 1):, COL['m_q']:COL['m_q'] + GROUP_W], gs))
        if last:
            y_p = nxt
        else:
            xn_p = nxt

        us = _inproj(xn_s, w_l)
        ust = us.T
        scores = _sample_scores(cache_kidx, page_table, us, ust[SMALL0:SMALL0 + LANES], l)
        pg, off, sbias, nbias = _sample_select(scores, us, topk_s)
        a_o = _sample_attn(cache_k, cache_v, page_table, us, pg, off, sbias, nbias, l)
        st = (state_ret[l], state_mlstm_c[l], state_mlstm_n[l], state_mlstm_m[l], state_mlstm_conv[l], state_gla[l])
        r_o, m_o, g_o, rs, mc, mn, mm, gs = _decode_step(us, ust, st, lw, past)
        hs, nxt = _out_stage(a_o, r_o, m_o, g_o, us, hs, p_sample[l].reshape(ms, PLE_DIM), lw, g_next, nd)
        conv_new = jnp.concatenate([state_mlstm_conv[l][:, 1:], us[:, None, COL['m_q']:COL['m_q'] + GROUP_W]], axis=1)
        outs_s.append((us[:, COL['a_k']:COL['a_k'] + GROUP_W].reshape(bs, ts, HEADS, HD),
                       us[:, COL['a_v']:COL['a_v'] + GROUP_W].reshape(bs, ts, HEADS, HD),
                       us[:, COL['a_ik']:COL['a_ik'] + IDX_DIM].reshape(bs, ts, IDX_DIM),
                       rs, mc, mn, mm, conv_new, gs))
        if last:
            y_s = nxt
        else:
            xn_s = nxt

    sp = [jnp.stack([o[j] for o in outs_p]) for j in range(9)]
    ss = [jnp.stack([o[j] for o in outs_s]) for j in range(9)]
    return (y_p.reshape(bp, tp, D_MODEL), y_s.reshape(bs, ts, D_MODEL), *sp, *ss)
```
